```python
import math
import jax
import jax.numpy as jnp
from jax import lax
import numpy as np

D_MODEL = 1024
BATCH = 32
SEQ = 2048
DEPTH = 4
DEC_BATCH = 16
DEC_SEQ = 16
PAST_LEN = 2048

CHUNK = 64
N_META = 16
Q_BLOCK = 128
EPS = 1e-6
NEG_INF = -1e30

DA_HEADS = 4
DA_HEAD_DIM = 64
DA_VDIM = 2 * DA_HEAD_DIM
DA_QK_WIDTH = DA_HEADS * 2 * DA_HEAD_DIM
DA_WIDTH = DA_HEADS * DA_VDIM

SSM_D_INNER = D_MODEL // 2
SSM_HEAD_DIM = 64
SSM_HEADS = SSM_D_INNER // SSM_HEAD_DIM
SSM_GROUPS = 2
SSM_D_STATE = 128
SSM_CONV = 4
SSM_CHUNK = 64
SSM_CONV_DIM = SSM_D_INNER + 2 * SSM_GROUPS * SSM_D_STATE

SB_HEADS = 8
SB_HEAD_DIM = 64
SB_WIDTH = SB_HEADS * SB_HEAD_DIM

D_MIX = DA_WIDTH + SSM_D_INNER + SB_WIDTH

D_FF = 2816
FFN_CONV = 3

REL_BUCKETS = 32
REL_MAX_DIST = 128

OFF_DA_K = DA_QK_WIDTH
OFF_DA_V = 2 * DA_QK_WIDTH
OFF_SSM_Z = 2 * DA_QK_WIDTH + DA_WIDTH
OFF_SSM_XBC = OFF_SSM_Z + SSM_D_INNER
OFF_SSM_DT = OFF_SSM_XBC + SSM_CONV_DIM
OFF_SB_Q = OFF_SSM_DT + SSM_HEADS
OFF_SB_K = OFF_SB_Q + SB_WIDTH
OFF_SB_V = OFF_SB_K + SB_WIDTH
D_IN_PROJ = OFF_SB_V + SB_WIDTH

kernel_name = 'hybrid_diffattn_ssd_stickbreaking_stream_step'


def rmsnorm(x, g):
    xf = x.astype(jnp.float32)
    y = xf * lax.rsqrt(jnp.mean(xf * xf, axis=-1, keepdims=True) + EPS)
    return (y * g.astype(jnp.float32)).astype(x.dtype)


def causal_dwconv(x_ext, w, b):
    y = lax.conv_general_dilated(x_ext, w[:, None, :].astype(x_ext.dtype), window_strides=(1,),
                                 padding='VALID', dimension_numbers=('NWC', 'WIO', 'NWC'),
                                 feature_group_count=x_ext.shape[-1])
    return y + b.astype(y.dtype)


def rel_bucket(rel):
    half = REL_BUCKETS // 2
    max_exact = half // 2
    n = jnp.abs(rel)
    nf = jnp.maximum(n, 1).astype(jnp.float32)
    large = max_exact + (jnp.log(nf / max_exact) / math.log(REL_MAX_DIST / max_exact)
                         * (half - max_exact)).astype(jnp.int32)
    large = jnp.minimum(large, half - 1)
    return jnp.where(rel > 0, half, 0) + jnp.where(n < max_exact, n, large)


def prompt_chunk_id(pos):
    return jnp.where(pos < N_META, -1, (pos - N_META) // CHUNK)


def to_blocks(a, n_blk):
    pad = n_blk * Q_BLOCK - a.shape[1]
    a = jnp.pad(a, [(0, 0), (0, pad)] + [(0, 0)] * (a.ndim - 2))
    return jnp.moveaxis(a.reshape((a.shape[0], n_blk, Q_BLOCK) + a.shape[2:]), 1, 0)


def from_blocks(a, L):
    a = jnp.moveaxis(a, 0, 1)
    return a.reshape((a.shape[0], -1) + a.shape[3:])[:, :L]


def diff_attn_block(q, k, v, q_pos, q_chunk, k_pos, k_chunk, rel_table, lam):
    scale = DA_HEAD_DIM ** -0.5
    bias = jnp.transpose(rel_table[rel_bucket(k_pos[None, :] - q_pos[:, None])], (2, 0, 1))
    mask = k_chunk[None, :] <= q_chunk[:, None]
    s = jnp.einsum('bqhmd,bkhmd->bmhqk', q, k).astype(jnp.float32) * scale
    s = jnp.where(mask, s + bias[None, None].astype(jnp.float32), NEG_INF)
    p = jax.nn.softmax(s, axis=-1)
    p = p[:, 0] - lam * p[:, 1]
    return jnp.einsum('bhqk,bkhv->bqhv', p.astype(v.dtype), v)


def stick_breaking_block(q, k, v, q_pos, k_pos):
    z = jnp.einsum('bqhd,bkhd->bhqk', q, k).astype(jnp.float32) * (SB_HEAD_DIM ** -0.5)
    mask = k_pos[None, :] < q_pos[:, None]
    log_keep = jnp.where(mask, jax.nn.log_sigmoid(-z), 0.0)
    tail = lax.cumsum(log_keep, axis=3, reverse=True) - log_keep
    w = jnp.where(mask, jnp.exp(jax.nn.log_sigmoid(z) + tail), 0.0)
    return jnp.einsum('bhqk,bkhd->bqhd', w.astype(v.dtype), v)


def ssd_scan(x, dt, a, bmat, cmat, h0):
    b, L = x.shape[:2]
    nc = -(-L // SSM_CHUNK)
    pad = nc * SSM_CHUNK - L
    R = SSM_HEADS // SSM_GROUPS

    def pad_l(t):
        return jnp.pad(t, [(0, 0), (0, pad)] + [(0, 0)] * (t.ndim - 2))

    xdt = pad_l(x.astype(jnp.float32) * dt[..., None]).reshape(b, nc, SSM_CHUNK, SSM_GROUPS, R, SSM_HEAD_DIM)
    ld = pad_l(dt * a).reshape(b, nc, SSM_CHUNK, SSM_GROUPS, R)
    bm = pad_l(bmat.astype(jnp.float32)).reshape(b, nc, SSM_CHUNK, SSM_GROUPS, SSM_D_STATE)
    cm = pad_l(cmat.astype(jnp.float32)).reshape(b, nc, SSM_CHUNK, SSM_GROUPS, SSM_D_STATE)
    cs = jnp.cumsum(jnp.transpose(ld, (0, 3, 4, 1, 2)), axis=-1)
    tril = jnp.tril(jnp.ones((SSM_CHUNK, SSM_CHUNK), dtype=bool))
    lmat = jnp.exp(jnp.where(tril, cs[..., :, None] - cs[..., None, :], -jnp.inf))
    cb = jnp.einsum('bclgn,bcsgn->bgcls', cm, bm)
    y_diag = jnp.einsum('bgcls,bgrcls,bcsgrp->bclgrp', cb, lmat, xdt)
    decay_to_end = jnp.exp(cs[..., -1:] - cs)
    chunk_states = jnp.einsum('bcsgn,bgrcs,bcsgrp->cbgrpn', bm, decay_to_end, xdt)
    chunk_decay = jnp.moveaxis(jnp.exp(cs[..., -1]), 3, 0)

    def step(h, inp):
        dec, st = inp
        return dec[..., None, None] * h + st, h

    h0g = h0.astype(jnp.float32).reshape(b, SSM_GROUPS, R, SSM_HEAD_DIM, SSM_D_STATE)
    h_final, h_prev = lax.scan(step, h0g, (chunk_decay, chunk_states))
    y_off = jnp.einsum('bclgn,cbgrpn,bgrcl->bclgrp', cm, h_prev, jnp.exp(cs))
    y = (y_diag + y_off).reshape(b, nc * SSM_CHUNK, SSM_HEADS, SSM_HEAD_DIM)[:, :L]
    h_final = h_final.reshape(b, SSM_HEADS, SSM_HEAD_DIM, SSM_D_STATE)
    return y.astype(x.dtype), h_final.astype(h0.dtype)


def layer(x, past, p, lam_init, rel_table):
    (w_in, w_out, norm_mix, norm_ffn, lq1, lk1, lq2, lk2, da_subln,
     conv_w, conv_b, dt_bias, a_log, d_skip, ssm_norm,
     w_gate, w_up, w_down, fconv_w, fconv_b) = p
    b, L, _ = x.shape
    h = rmsnorm(x, norm_mix)
    proj = h @ w_in
    da_q, da_k, da_v, ssm_z, ssm_xbc, ssm_dt, sb_q, sb_k, sb_v = jnp.split(
        proj, [OFF_DA_K, OFF_DA_V, OFF_SSM_Z, OFF_SSM_XBC, OFF_SSM_DT, OFF_SB_Q, OFF_SB_K, OFF_SB_V], axis=-1)
    da_q = da_q.reshape(b, L, DA_HEADS, 2, DA_HEAD_DIM)
    da_k = da_k.reshape(b, L, DA_HEADS, 2, DA_HEAD_DIM)
    da_v = da_v.reshape(b, L, DA_HEADS, DA_VDIM)
    sb_q = sb_q.reshape(b, L, SB_HEADS, SB_HEAD_DIM)
    sb_k = sb_k.reshape(b, L, SB_HEADS, SB_HEAD_DIM)
    sb_v = sb_v.reshape(b, L, SB_HEADS, SB_HEAD_DIM)
    lam = (jnp.exp(jnp.sum(lq1.astype(jnp.float32) * lk1.astype(jnp.float32)))
           - jnp.exp(jnp.sum(lq2.astype(jnp.float32) * lk2.astype(jnp.float32))) + lam_init)

    if past is None:
        pos = jnp.arange(L, dtype=jnp.int32)
        chunk = prompt_chunk_id(pos)
        n_blk = -(-L // Q_BLOCK)
        qpos_b = jnp.arange(n_blk * Q_BLOCK, dtype=jnp.int32).reshape(n_blk, Q_BLOCK)
        da_o = from_blocks(lax.map(
            lambda blk: diff_attn_block(blk[0], da_k, da_v, blk[1], prompt_chunk_id(blk[1]), pos, chunk, rel_table, lam),
            (to_blocks(da_q, n_blk), qpos_b)), L)
        sb_o = from_blocks(lax.map(
            lambda blk: stick_breaking_block(blk[0], sb_k, sb_v, blk[1], pos),
            (to_blocks(sb_q, n_blk), qpos_b)), L)
        conv_in = jnp.pad(ssm_xbc, ((0, 0), (SSM_CONV - 1, 0), (0, 0)))
        h0 = jnp.zeros((b, SSM_HEADS, SSM_HEAD_DIM, SSM_D_STATE), x.dtype)
    else:
        c_da_k, c_da_v, c_sb_k, c_sb_v, h0, conv_hist, _ = past
        past_len = c_da_k.shape[1]
        kpos = jnp.arange(past_len + L, dtype=jnp.int32)
        qpos = kpos[past_len:]
        da_o = diff_attn_block(da_q, jnp.concatenate([c_da_k, da_k], axis=1), jnp.concatenate([c_da_v, da_v], axis=1),
                               qpos, qpos // CHUNK, kpos, kpos // CHUNK, rel_table, lam)
        sb_o = stick_breaking_block(sb_q, jnp.concatenate([c_sb_k, sb_k], axis=1),
                                    jnp.concatenate([c_sb_v, sb_v], axis=1), qpos, kpos)
        conv_in = jnp.concatenate([conv_hist, ssm_xbc], axis=1)

    da_o = (rmsnorm(da_o, da_subln) * (1.0 - lam_init)).reshape(b, L, DA_WIDTH)

    new_conv = conv_in[:, -(SSM_CONV - 1):]
    xbc = jax.nn.silu(causal_dwconv(conv_in, conv_w, conv_b))
    ssm_x, ssm_b, ssm_c = jnp.split(xbc, [SSM_D_INNER, SSM_D_INNER + SSM_GROUPS * SSM_D_STATE], axis=-1)
    ssm_x = ssm_x.reshape(b, L, SSM_HEADS, SSM_HEAD_DIM)
    dt = jax.nn.softplus(ssm_dt.astype(jnp.float32) + dt_bias.astype(jnp.float32))
    a = -jnp.exp(a_log.astype(jnp.float32))
    y, h_final = ssd_scan(ssm_x, dt, a, ssm_b.reshape(b, L, SSM_GROUPS, SSM_D_STATE),
                          ssm_c.reshape(b, L, SSM_GROUPS, SSM_D_STATE), h0)
    y = (y + d_skip[:, None] * ssm_x).reshape(b, L, SSM_D_INNER) * jax.nn.silu(ssm_z)
    y = rmsnorm(y.reshape(b, L, SSM_GROUPS, SSM_D_INNER // SSM_GROUPS),
                ssm_norm.reshape(SSM_GROUPS, SSM_D_INNER // SSM_GROUPS)).reshape(b, L, SSM_D_INNER)

    mix = jnp.concatenate([da_o, y, sb_o.reshape(b, L, SB_WIDTH)], axis=-1)
    x = x + mix @ w_out

    h2 = rmsnorm(x, norm_ffn)
    g = h2 @ w_gate
    u = h2 @ w_up
    if past is None:
        g_ext = jnp.pad(g, ((0, 0), (FFN_CONV - 1, 0), (0, 0)))
    else:
        g_ext = jnp.concatenate([past[6], g], axis=1)
    new_ffn = g_ext[:, -(FFN_CONV - 1):]
    g = causal_dwconv(g_ext, fconv_w, fconv_b)
    x = x + (jax.nn.silu(g) * u) @ w_down
    return x, (da_k, da_v, sb_k, sb_v, h_final, new_conv, new_ffn)


def stack_states(states):
    return tuple(jnp.stack([s[j] for s in states], axis=0) for j in range(7))


def setup_inputs(seed: int = 0) -> dict:
    key = jax.random.key(seed)
    ks = jax.random.split(key, 40)

    def nrm(i, shape, scale=1.0):
        return jax.random.normal(ks[i], shape, jnp.float32) * scale

    dt0 = jnp.exp(jax.random.uniform(ks[20], (DEPTH, SSM_HEADS), jnp.float32,
                                     math.log(0.001), math.log(0.1)))
    dt_bias = dt0 + jnp.log(-jnp.expm1(-dt0))
    a_log = jnp.log(jax.random.uniform(ks[21], (DEPTH, SSM_HEADS), jnp.float32, 1.0, 16.0))
    return {
        'x_prompt': nrm(0, (BATCH, SEQ, D_MODEL)),
        'x_sample': nrm(1, (DEC_BATCH, DEC_SEQ, D_MODEL)),
        'cache_da_k': nrm(2, (DEPTH, DEC_BATCH, PAST_LEN, DA_HEADS, 2, DA_HEAD_DIM)),
        'cache_da_v': nrm(3, (DEPTH, DEC_BATCH, PAST_LEN, DA_HEADS, DA_VDIM)),
        'cache_sb_k': nrm(4, (DEPTH, DEC_BATCH, PAST_LEN, SB_HEADS, SB_HEAD_DIM)),
        'cache_sb_v': nrm(5, (DEPTH, DEC_BATCH, PAST_LEN, SB_HEADS, SB_HEAD_DIM)),
        'state_ssm': nrm(6, (DEPTH, DEC_BATCH, SSM_HEADS, SSM_HEAD_DIM, SSM_D_STATE), 0.5),
        'state_ssm_conv': nrm(7, (DEPTH, DEC_BATCH, SSM_CONV - 1, SSM_CONV_DIM)),
        'state_ffn_conv': nrm(8, (DEPTH, DEC_BATCH, FFN_CONV - 1, D_FF)),
        'meta_tokens': nrm(9, (N_META, D_MODEL)),
        'rel_bias_table': nrm(10, (REL_BUCKETS, DA_HEADS), 0.5),
        'w_in': nrm(11, (DEPTH, D_MODEL, D_IN_PROJ), D_MODEL ** -0.5),
        'w_out': nrm(12, (DEPTH, D_MIX, D_MODEL), D_MIX ** -0.5),
        'norm_mix': 1.0 + nrm(13, (DEPTH, D_MODEL), 0.05),
        'norm_ffn': 1.0 + nrm(14, (DEPTH, D_MODEL), 0.05),
        'da_lambda_q1': nrm(15, (DEPTH, DA_HEAD_DIM), 0.1),
        'da_lambda_k1': nrm(16, (DEPTH, DA_HEAD_DIM), 0.1),
        'da_lambda_q2': nrm(17, (DEPTH, DA_HEAD_DIM), 0.1),
        'da_lambda_k2': nrm(18, (DEPTH, DA_HEAD_DIM), 0.1),
        'da_subln': 1.0 + nrm(19, (DEPTH, DA_VDIM), 0.05),
        'ssm_conv_w': nrm(22, (DEPTH, SSM_CONV, SSM_CONV_DIM), SSM_CONV ** -0.5),
        'ssm_conv_b': nrm(23, (DEPTH, SSM_CONV_DIM), 0.01),
        'ssm_dt_bias': dt_bias,
        'ssm_a_log': a_log,
        'ssm_d': 1.0 + nrm(24, (DEPTH, SSM_HEADS), 0.05),
        'ssm_norm': 1.0 + nrm(25, (DEPTH, SSM_D_INNER), 0.05),
        'ffn_w_gate': nrm(26, (DEPTH, D_MODEL, D_FF), D_MODEL ** -0.5),
        'ffn_w_up': nrm(27, (DEPTH, D_MODEL, D_FF), D_MODEL ** -0.5),
        'ffn_w_down': nrm(28, (DEPTH, D_FF, D_MODEL), D_FF ** -0.5),
        'ffn_conv_w': nrm(29, (DEPTH, FFN_CONV, D_FF), FFN_CONV ** -0.5),
        'ffn_conv_b': nrm(30, (DEPTH, D_FF), 0.01),
        'final_norm': 1.0 + nrm(31, (D_MODEL,), 0.05),
    }


def reference(x_prompt, x_sample, cache_da_k, cache_da_v, cache_sb_k, cache_sb_v, state_ssm,
              state_ssm_conv, state_ffn_conv, meta_tokens, rel_bias_table, w_in, w_out, norm_mix,
              norm_ffn, da_lambda_q1, da_lambda_k1, da_lambda_q2, da_lambda_k2, da_subln, ssm_conv_w,
              ssm_conv_b, ssm_dt_bias, ssm_a_log, ssm_d, ssm_norm, ffn_w_gate, ffn_w_up, ffn_w_down,
              ffn_conv_w, ffn_conv_b, final_norm):
    b = x_prompt.shape[0]
    meta = jnp.broadcast_to(meta_tokens[None].astype(x_prompt.dtype), (b, N_META, D_MODEL))
    xp = jnp.concatenate([meta, x_prompt], axis=1)
    xs = x_sample
    p_states = []
    s_states = []
    for i in range(DEPTH):
        params = (w_in[i], w_out[i], norm_mix[i], norm_ffn[i], da_lambda_q1[i], da_lambda_k1[i],
                  da_lambda_q2[i], da_lambda_k2[i], da_subln[i], ssm_conv_w[i], ssm_conv_b[i],
                  ssm_dt_bias[i], ssm_a_log[i], ssm_d[i], ssm_norm[i], ffn_w_gate[i], ffn_w_up[i],
                  ffn_w_down[i], ffn_conv_w[i], ffn_conv_b[i])
        lam_init = 0.8 - 0.6 * math.exp(-0.3 * i)
        xp, st_p = layer(xp, None, params, lam_init, rel_bias_table)
        past = (cache_da_k[i], cache_da_v[i], cache_sb_k[i], cache_sb_v[i], state_ssm[i],
                state_ssm_conv[i], state_ffn_conv[i])
        xs, st_s = layer(xs, past, params, lam_init, rel_bias_table)
        p_states.append(st_p)
        s_states.append(st_s)
    y_prompt = rmsnorm(xp[:, N_META:], final_norm)
    y_sample = rmsnorm(xs, final_norm)
    p_da_k, p_da_v, p_sb_k, p_sb_v, p_ssm, p_ssm_conv, p_ffn_conv = stack_states(p_states)
    s_da_k, s_da_v, s_sb_k, s_sb_v, s_ssm, s_ssm_conv, s_ffn_conv = stack_states(s_states)
    return (y_prompt, y_sample, p_da_k, p_da_v, p_sb_k, p_sb_v, p_ssm, p_ssm_conv, p_ffn_conv,
            s_da_k, s_da_v, s_sb_k, s_sb_v, s_ssm, s_ssm_conv, s_ffn_conv)
```

```python
import functools
import math

import jax
import jax.numpy as jnp
from jax import lax
from jax.experimental import pallas as pl
from jax.experimental.pallas import tpu as pltpu

F32 = jnp.float32
BF16 = jnp.bfloat16

LANES = 128
SUBLANES = 8
BF16_ROWS = 16
V7X_VMEM_BYTES = 64 * 1024 * 1024
VMEM_LIMIT = 56 * 1024 * 1024

CHUNK = 64
N_META = 16
EPS = 1e-6
NEG_INF = -1e30
DA_HEADS = 4
DA_HEAD_DIM = 64
DA_VDIM = 128
SB_HEADS = 8
SB_HEAD_DIM = 64
SSM_HEADS = 8
SSM_HEAD_DIM = 64
SSM_GROUPS = 2
SSM_D_STATE = 128
SSM_CONV = 4
FFN_CONV = 3
REL_BUCKETS = 32
REL_MAX_DIST = 128
QK_SCALE = 0.125
SSD_BLOCK = 128
FF_CHUNK = 256


def _dot(a, b):
    return jnp.dot(a, b, preferred_element_type=F32)


def _dot_nt(a, b):
    return lax.dot_general(a, b, (((1,), (1,)), ((), ())), preferred_element_type=F32)


def _split3(x):
    hi = x.astype(BF16)
    r1 = x - hi.astype(F32)
    mid = r1.astype(BF16)
    lo = (r1 - mid.astype(F32)).astype(BF16)
    return hi, mid, lo


def _dot_exact_rhs(x, m_bf16):
    hi, mid, lo = _split3(x)
    return _dot(hi, m_bf16) + _dot(mid, m_bf16) + _dot(lo, m_bf16)


def _dot_exact_lhs(m_bf16, x):
    hi, mid, lo = _split3(x)
    return _dot(m_bf16, hi) + _dot(m_bf16, mid) + _dot(m_bf16, lo)


def _rms(x, g):
    return x * lax.rsqrt(jnp.mean(x * x, axis=-1, keepdims=True) + EPS) * g


def _silu(x):
    return x * (1.0 / (1.0 + jnp.exp(-x)))


def _softplus(x):
    return jnp.maximum(x, 0.0) + jnp.log1p(jnp.exp(-jnp.abs(x)))


def _pick_tile(n, cap, mult):
    best = None
    for t in range(mult, min(n, cap) + 1, mult):
        if n % t == 0:
            best = t
    assert best is not None, (n, cap, mult)
    return best


def _resident(shape):
    nd = len(shape)
    return pl.BlockSpec(shape, lambda *_: (0,) * nd, pipeline_mode=pl.Buffered(1))


def _params(sem):
    return pltpu.CompilerParams(dimension_semantics=sem, vmem_limit_bytes=VMEM_LIMIT)


def _rel_bucket(rel):
    half = REL_BUCKETS // 2
    max_exact = half // 2
    n = jnp.abs(rel)
    nf = jnp.maximum(n, 1).astype(jnp.float32)
    large = max_exact + (jnp.log(nf / max_exact) / math.log(REL_MAX_DIST / max_exact)
                         * (half - max_exact)).astype(jnp.int32)
    large = jnp.minimum(large, half - 1)
    return jnp.where(rel > 0, half, 0) + jnp.where(n < max_exact, n, large)


def _bias_kernel(tab_ref, bp_ref, bs_ref, op_ref, os_ref, *, past, n_new):
    h = pl.program_id(0)

    def lookup(bucket):
        val = jnp.zeros(bucket.shape, F32)
        for b in range(REL_BUCKETS):
            val = jnp.where(bucket == b, tab_ref[b, h], val)
        return val

    r = lax.broadcasted_iota(jnp.int32, (LANES, LANES), 0)
    c = lax.broadcasted_iota(jnp.int32, (LANES, LANES), 1)
    qc = jnp.right_shift(r - N_META, 6)
    kc = jnp.right_shift(c - N_META, 6)
    for slot in range(4):
        val = lookup(bp_ref[slot])
        if slot > 0:
            d = slot - 2
            val = jnp.where(2 * d + kc - qc <= 0, val, NEG_INF)
        op_ref[0, slot] = val

    lw = bs_ref.shape[1]
    rs = lax.broadcasted_iota(jnp.int32, (n_new, lw), 0) + past
    cs = lax.broadcasted_iota(jnp.int32, (n_new, lw), 1)
    vis = (jnp.right_shift(cs, 6) <= jnp.right_shift(rs, 6)) & (cs < past + n_new)
    os_ref[0] = jnp.where(vis, lookup(bs_ref[...]), NEG_INF)


def _bias_tiles(rel_table, past, n_new):
    lw = pl.cdiv(past + n_new, LANES) * LANES
    r = jnp.arange(LANES, dtype=jnp.int32)[:, None]
    c = jnp.arange(LANES, dtype=jnp.int32)[None, :]
    far = jnp.full((LANES, LANES), -2 * LANES, jnp.int32)
    bp = jnp.stack([_rel_bucket(far)] + [_rel_bucket(d * LANES + c - r) for d in (-1, 0, 1)])
    qs = past + jnp.arange(n_new, dtype=jnp.int32)[:, None]
    ks = jnp.arange(lw, dtype=jnp.int32)[None, :]
    bs = _rel_bucket(ks - qs)
    return pl.pallas_call(
        functools.partial(_bias_kernel, past=past, n_new=n_new),
        grid=(DA_HEADS,),
        in_specs=[pl.BlockSpec(memory_space=pltpu.SMEM),
                  pl.BlockSpec((4, LANES, LANES), lambda h: (0, 0, 0)),
                  pl.BlockSpec((n_new, lw), lambda h: (0, 0))],
        out_specs=[pl.BlockSpec((1, 4, LANES, LANES), lambda h: (h, 0, 0, 0)),
                   pl.BlockSpec((1, n_new, lw), lambda h: (h, 0, 0))],
        out_shape=[jax.ShapeDtypeStruct((DA_HEADS, 4, LANES, LANES), F32),
                   jax.ShapeDtypeStruct((DA_HEADS, n_new, lw), F32)],
        name="rel_bias_tiles",
    )(rel_table, bp, bs)


_SEG = (("da_q", 512, BF16), ("da_k", 512, F32), ("da_v", 512, F32), ("ssm_z", 512, F32),
        ("ssm_xbc", 1024, F32), ("sb_q", 512, BF16), ("sb_k", 512, F32), ("sb_v", 512, F32),
        ("ssm_dt", LANES, F32))
_N_PROJ = sum(s[1] for s in _SEG)
_OFF_DT = 3072


def _inproj_kernel(x_ref, g_ref, w_ref, *out_refs):
    h = _rms(x_ref[...], g_ref[...]).astype(BF16)
    off = 0
    for (_, n, dt), o_ref in zip(_SEG, out_refs):
        o_ref[...] = _dot(h, w_ref[:, off:off + n]).astype(dt)
        off += n


def _in_proj(x2d, g, w_r):
    t, d = x2d.shape
    tm = _pick_tile(t, 256, BF16_ROWS)
    return pl.pallas_call(
        _inproj_kernel,
        grid=(t // tm,),
        in_specs=[pl.BlockSpec((tm, d), lambda i: (i, 0)), _resident((1, d)), _resident((d, _N_PROJ))],
        out_specs=[pl.BlockSpec((tm, n), lambda i: (i, 0)) for _, n, _ in _SEG],
        out_shape=[jax.ShapeDtypeStruct((t, n), dt) for _, n, dt in _SEG],
        compiler_params=_params(("arbitrary",)),
        name="in_proj",
    )(x2d, g, w_r)


def _lambda(lamp_ref, lam_init):
    lp = lamp_ref[...]
    return (jnp.exp(jnp.sum(lp[0:1] * lp[1:2], axis=1, keepdims=True))
            - jnp.exp(jnp.sum(lp[2:3] * lp[3:4], axis=1, keepdims=True)) + lam_init)


def _split_maps(q_bf16):
    lane = lax.broadcasted_iota(jnp.int32, (1, LANES), 1)
    qf = q_bf16.astype(F32) * QK_SCALE
    return (jnp.where(lane < DA_HEAD_DIM, qf, 0.0).astype(BF16),
            jnp.where(lane >= DA_HEAD_DIM, qf, 0.0).astype(BF16))


def _da_prompt_kernel(q_ref, k_ref, v_ref, bm_ref, lamp_ref, subln_ref, o_ref,
                      q1s, q2s, ks, vs, sbuf, *, seq, lam_init):
    n_full, tail = divmod(seq, LANES)
    n_blk = n_full + (1 if tail else 0)
    lp = n_blk * LANES
    q1, q2 = _split_maps(q_ref[0])
    q1s[...] = q1
    q2s[...] = q2
    ks[0:seq, :] = k_ref[0].astype(BF16)
    vs[0:seq, :] = v_ref[0].astype(BF16)
    if lp > seq:
        ks[seq:lp, :] = jnp.zeros((lp - seq, LANES), BF16)
        vs[seq:lp, :] = jnp.zeros((lp - seq, LANES), BF16)
    lam = _lambda(lamp_ref, lam_init)
    gsub = subln_ref[...] * (1.0 - lam_init)

    def tile(i, rows, r0):
        q1t = q1s[pl.ds(r0, rows), :]
        q2t = q2s[pl.ds(r0, rows), :]
        nkb = jnp.minimum(i + 2, n_blk)

        def scores(kb, carry):
            m1, m2 = carry
            kblk = ks[pl.ds(pl.multiple_of(kb * LANES, LANES), LANES), :]
            bias = bm_ref[0, jnp.clip(kb - i + 2, 0, 3), 0:rows, :]
            s1 = _dot_nt(q1t, kblk) + bias
            s2 = _dot_nt(q2t, kblk) + bias
            sbuf[0, kb, 0:rows, :] = s1
            sbuf[1, kb, 0:rows, :] = s2
            return jnp.maximum(m1, s1), jnp.maximum(m2, s2)

        ninf = jnp.full((rows, LANES), -jnp.inf, F32)
        m1, m2 = lax.fori_loop(0, nkb, scores, (ninf, ninf))
        m1 = jnp.max(m1, axis=1, keepdims=True)
        m2 = jnp.max(m2, axis=1, keepdims=True)

        def expsum(kb, carry):
            l1, l2 = carry
            p1 = jnp.exp(sbuf[0, kb, 0:rows, :] - m1)
            p2 = jnp.exp(sbuf[1, kb, 0:rows, :] - m2)
            sbuf[0, kb, 0:rows, :] = p1
            sbuf[1, kb, 0:rows, :] = p2
            return l1 + p1, l2 + p2

        zero = jnp.zeros((rows, LANES), F32)
        l1, l2 = lax.fori_loop(0, nkb, expsum, (zero, zero))
        c1 = 1.0 / jnp.sum(l1, axis=1, keepdims=True)
        c2 = lam / jnp.sum(l2, axis=1, keepdims=True)

        def pv(kb, acc):
            p = (sbuf[0, kb, 0:rows, :] * c1 - sbuf[1, kb, 0:rows, :] * c2).astype(BF16)
            vblk = vs[pl.ds(pl.multiple_of(kb * LANES, LANES), LANES), :]
            return acc + _dot(p, vblk)

        acc = lax.fori_loop(0, nkb, pv, zero)
        o_ref[0, pl.ds(r0, rows), :] = _rms(acc, gsub).astype(o_ref.dtype)

    def full_tile(i, carry):
        tile(i, LANES, pl.multiple_of(i * LANES, LANES))
        return carry

    lax.fori_loop(0, n_full, full_tile, 0)
    if tail:
        tile(n_full, tail, n_full * LANES)


def _da_prompt(q, k, v, bm, lamp, subln, lam_init):
    b, seq, _ = q.shape
    assert seq % BF16_ROWS == 0
    n_blk = pl.cdiv(seq, LANES)
    lp = n_blk * LANES
    blk = lambda: pl.BlockSpec((1, seq, LANES), lambda i, h: (i, 0, h))
    return pl.pallas_call(
        functools.partial(_da_prompt_kernel, seq=seq, lam_init=lam_init),
        grid=(b, DA_HEADS),
        in_specs=[blk(), blk(), blk(),
                  pl.BlockSpec((1, 4, LANES, LANES), lambda i, h: (h, 0, 0, 0)),
                  pl.BlockSpec((4, DA_HEAD_DIM), lambda i, h: (0, 0)),
                  pl.BlockSpec((1, DA_VDIM), lambda i, h: (0, 0))],
        out_specs=blk(),
        out_shape=jax.ShapeDtypeStruct((b, seq, DA_HEADS * DA_VDIM), BF16),
        scratch_shapes=[pltpu.VMEM((seq, LANES), BF16), pltpu.VMEM((seq, LANES), BF16),
                        pltpu.VMEM((lp, LANES), BF16), pltpu.VMEM((lp, LANES), BF16),
                        pltpu.VMEM((2, n_blk, LANES, LANES), F32)],
        compiler_params=_params(("arbitrary", "arbitrary")),
        name="diff_attn_prompt",
    )(q, k, v, bm, lamp, subln)


def _da_sample_kernel(q_ref, k_ref, v_ref, ck_ref, cv_ref, bs_ref, lamp_ref, subln_ref, o_ref,
                      ks, vs, *, past, n_new, lam_init):
    lw = ks.shape[0]
    tot = past + n_new
    ks[0:past, :] = ck_ref[0, 0].astype(BF16)
    vs[0:past, :] = cv_ref[0, 0].astype(BF16)
    ks[past:tot, :] = k_ref[0].astype(BF16)
    vs[past:tot, :] = v_ref[0].astype(BF16)
    if lw > tot:
        ks[tot:lw, :] = jnp.zeros((lw - tot, LANES), BF16)
        vs[tot:lw, :] = jnp.zeros((lw - tot, LANES), BF16)
    q1, q2 = _split_maps(q_ref[0])
    bias = bs_ref[0]
    kall = ks[...]
    s1 = _dot_nt(q1, kall) + bias
    s2 = _dot_nt(q2, kall) + bias
    p1 = jnp.exp(s1 - jnp.max(s1, axis=1, keepdims=True))
    p2 = jnp.exp(s2 - jnp.max(s2, axis=1, keepdims=True))
    c1 = 1.0 / jnp.sum(p1, axis=1, keepdims=True)
    c2 = _lambda(lamp_ref, lam_init) / jnp.sum(p2, axis=1, keepdims=True)
    acc = _dot((p1 * c1 - p2 * c2).astype(BF16), vs[...])
    o_ref[0] = _rms(acc, subln_ref[...] * (1.0 - lam_init)).astype(o_ref.dtype)


def _da_sample(q, k, v, cache_k, cache_v, layer, bs, lamp, subln, lam_init):
    b, n_new, _ = q.shape
    past = cache_k.shape[2]
    assert past % BF16_ROWS == 0 and n_new % BF16_ROWS == 0
    lw = bs.shape[2]
    new = lambda: pl.BlockSpec((1, n_new, LANES), lambda i, h: (i, 0, h))
    old = lambda: pl.BlockSpec((1, 1, past, LANES), lambda i, h: (layer, i, 0, h))
    return pl.pallas_call(
        functools.partial(_da_sample_kernel, past=past, n_new=n_new, lam_init=lam_init),
        grid=(b, DA_HEADS),
        in_specs=[new(), new(), new(), old(), old(),
                  pl.BlockSpec((1, n_new, lw), lambda i, h: (h, 0, 0)),
                  pl.BlockSpec((4, DA_HEAD_DIM), lambda i, h: (0, 0)),
                  pl.BlockSpec((1, DA_VDIM), lambda i, h: (0, 0))],
        out_specs=new(),
        out_shape=jax.ShapeDtypeStruct((b, n_new, DA_HEADS * DA_VDIM), BF16),
        scratch_shapes=[pltpu.VMEM((lw, LANES), BF16), pltpu.VMEM((lw, LANES), BF16)],
        compiler_params=_params(("arbitrary", "arbitrary")),
        name="diff_attn_sample",
    )(q, k, v, cache_k, cache_v, bs, lamp, subln)


def _split_heads(q_bf16):
    lane = lax.broadcasted_iota(jnp.int32, (1, LANES), 1)
    qf = q_bf16.astype(F32) * QK_SCALE
    return (jnp.where(lane < SB_HEAD_DIM, qf, 0.0).astype(BF16),
            jnp.where(lane >= SB_HEAD_DIM, qf, 0.0).astype(BF16))


def _sb_tile(i, rows, qts, ks, vs, upper):
    row = lax.broadcasted_iota(jnp.int32, (rows, LANES), 0)
    col = lax.broadcasted_iota(jnp.int32, (rows, LANES), 1)
    vis = col < row
    outs = []
    for qt in qts:
        def block(kb, run, acc, diagonal):
            c0 = pl.multiple_of(kb * LANES, LANES)
            z = _dot_nt(qt, ks[pl.ds(c0, LANES), :])
            log_beta = jnp.minimum(z, 0.0) - jnp.log1p(jnp.exp(-jnp.abs(z)))
            log_keep = log_beta - z
            if diagonal:
                log_keep = jnp.where(vis, log_keep, 0.0)
            hi = log_keep.astype(BF16)
            lo = (log_keep - hi.astype(F32)).astype(BF16)
            inner = _dot(hi, upper) + _dot(lo, upper)
            w = jnp.exp(log_beta + inner + run)
            if diagonal:
                w = jnp.where(vis, w, 0.0)
            acc = acc + _dot(w.astype(BF16), vs[pl.ds(c0, LANES), :])
            return run + jnp.sum(log_keep, axis=1, keepdims=True), acc

        run, acc = block(i, jnp.zeros((rows, 1), F32), jnp.zeros((rows, LANES), F32), True)

        def body(t, carry):
            return block(i - 1 - t, carry[0], carry[1], False)

        run, acc = lax.fori_loop(0, i, body, (run, acc))
        outs.append(acc)
    lane = lax.broadcasted_iota(jnp.int32, (1, LANES), 1)
    return jnp.where(lane < SB_HEAD_DIM, outs[0], outs[1])


def _upper_ones():
    r = lax.broadcasted_iota(jnp.int32, (LANES, LANES), 0)
    c = lax.broadcasted_iota(jnp.int32, (LANES, LANES), 1)
    return jnp.where(r > c, 1.0, 0.0).astype(BF16)


def _sb_prompt_kernel(q_ref, k_ref, v_ref, o_ref, qas, qbs, ks, vs, *, seq):
    n_full, tail = divmod(seq, LANES)
    lp = ks.shape[0]
    qa, qb = _split_heads(q_ref[0])
    qas[...] = qa
    qbs[...] = qb
    ks[0:seq, :] = k_ref[0].astype(BF16)
    vs[0:seq, :] = v_ref[0].astype(BF16)
    if lp > seq:
        ks[seq:lp, :] = jnp.zeros((lp - seq, LANES), BF16)
        vs[seq:lp, :] = jnp.zeros((lp - seq, LANES), BF16)
    upper = _upper_ones()

    def full_tile(i, carry):
        r0 = pl.multiple_of(i * LANES, LANES)
        o = _sb_tile(i, LANES, (qas[pl.ds(r0, LANES), :], qbs[pl.ds(r0, LANES), :]), ks, vs, upper)
        o_ref[0, pl.ds(r0, LANES), :] = o.astype(o_ref.dtype)
        return carry

    lax.fori_loop(0, n_full, full_tile, 0)
    if tail:
        r0 = n_full * LANES
        o = _sb_tile(n_full, tail, (qas[r0:r0 + tail, :], qbs[r0:r0 + tail, :]), ks, vs, upper)
        o_ref[0, r0:r0 + tail, :] = o.astype(o_ref.dtype)


def _sb_prompt(q, k, v):
    b, seq, _ = q.shape
    assert seq % BF16_ROWS == 0
    lp = pl.cdiv(seq, LANES) * LANES
    blk = lambda: pl.BlockSpec((1, seq, LANES), lambda i, p: (i, 0, p))
    return pl.pallas_call(
        functools.partial(_sb_prompt_kernel, seq=seq),
        grid=(b, SB_HEADS // 2),
        in_specs=[blk(), blk(), blk()],
        out_specs=blk(),
        out_shape=jax.ShapeDtypeStruct((b, seq, SB_HEADS * SB_HEAD_DIM), BF16),
        scratch_shapes=[pltpu.VMEM((seq, LANES), BF16), pltpu.VMEM((seq, LANES), BF16),
                        pltpu.VMEM((lp, LANES), BF16), pltpu.VMEM((lp, LANES), BF16)],
        compiler_params=_params(("arbitrary", "arbitrary")),
        name="stick_prompt",
    )(q, k, v)


def _sb_sample_kernel(q_ref, k_ref, v_ref, ck_ref, cv_ref, o_ref, ks, vs, *, past, n_new):
    lw = ks.shape[0]
    tot = past + n_new
    ks[0:past, :] = ck_ref[0, 0].astype(BF16)
    vs[0:past, :] = cv_ref[0, 0].astype(BF16)
    ks[past:tot, :] = k_ref[0].astype(BF16)
    vs[past:tot, :] = v_ref[0].astype(BF16)
    if lw > tot:
        ks[tot:lw, :] = jnp.zeros((lw - tot, LANES), BF16)
        vs[tot:lw, :] = jnp.zeros((lw - tot, LANES), BF16)
    o = _sb_tile(past // LANES, n_new, _split_heads(q_ref[0]), ks, vs, _upper_ones())
    o_ref[0] = o.astype(o_ref.dtype)


def _sb_sample(q, k, v, cache_k, cache_v, layer):
    b, n_new, _ = q.shape
    past = cache_k.shape[2]
    assert past % LANES == 0 and n_new <= LANES and n_new % BF16_ROWS == 0
    lw = past + LANES
    new = lambda: pl.BlockSpec((1, n_new, LANES), lambda i, p: (i, 0, p))
    old = lambda: pl.BlockSpec((1, 1, past, LANES), lambda i, p: (layer, i, 0, p))
    return pl.pallas_call(
        functools.partial(_sb_sample_kernel, past=past, n_new=n_new),
        grid=(b, SB_HEADS // 2),
        in_specs=[new(), new(), new(), old(), old()],
        out_specs=new(),
        out_shape=jax.ShapeDtypeStruct((b, n_new, SB_HEADS * SB_HEAD_DIM), BF16),
        scratch_shapes=[pltpu.VMEM((lw, LANES), BF16), pltpu.VMEM((lw, LANES), BF16)],
        compiler_params=_params(("arbitrary", "arbitrary")),
        name="stick_sample",
    )(q, k, v, cache_k, cache_v)


def _ssd_kernel(xbc_ref, z_ref, dt_ref, hist_ref, h0_ref, cw_ref, cb_ref, dtb_ref, alog_ref,
                dexp_ref, gn_ref, y_ref, hfin_ref, prev8, ht, *, seq):
    d_in = SSM_HEADS * SSM_HEAD_DIM
    gw = d_in // SSM_GROUPS
    n_full, tail = divmod(seq, SSD_BLOCK)
    blk = SSD_BLOCK

    r = lax.broadcasted_iota(jnp.int32, (blk, blk), 0)
    c = lax.broadcasted_iota(jnp.int32, (blk, blk), 1)
    tril = r >= c
    tri = jnp.where(tril, 1.0, 0.0).astype(BF16)
    er = lax.broadcasted_iota(jnp.int32, (LANES, d_in), 0)
    ec = lax.broadcasted_iota(jnp.int32, (LANES, d_in), 1)
    expand = jnp.where(jnp.right_shift(ec, 6) == er, 1.0, 0.0).astype(BF16)
    lane = lax.broadcasted_iota(jnp.int32, (1, LANES), 1)

    prev8[...] = hist_ref[0]
    for g in range(SSM_GROUPS):
        for j in range(gw // LANES):
            rows = h0_ref[0, g * gw + j * LANES:g * gw + (j + 1) * LANES, :]
            ht[g, :, j * LANES:(j + 1) * LANES] = rows.T

    a_neg = -jnp.exp(alog_ref[...])
    cw = cw_ref[...]

    def block(t0, rows):
        xin = xbc_ref[0, pl.ds(t0, rows), :]
        win = jnp.concatenate([prev8[...], xin], axis=0)
        conv = cw[3:4] * win
        for k in range(1, SSM_CONV):
            conv = conv + cw[3 - k:4 - k] * pltpu.roll(win, k, axis=0)
        xbc = _silu(conv[SUBLANES:] + cb_ref[...])
        prev8[...] = xin[rows - SUBLANES:rows]
        dt = _softplus(dt_ref[0, pl.ds(t0, rows), :] + dtb_ref[...])
        z = z_ref[0, pl.ds(t0, rows), :]
        if rows < blk:
            pad = blk - rows
            xbc = jnp.concatenate([xbc, jnp.zeros((pad, xbc.shape[1]), F32)], axis=0)
            dt = jnp.concatenate([dt, jnp.zeros((pad, LANES), F32)], axis=0)
            z = jnp.concatenate([z, jnp.zeros((pad, d_in), F32)], axis=0)
        x = xbc[:, 0:d_in]
        cs = _dot_exact_lhs(tri, dt * a_neg)
        cs_t = cs.T
        dt_e = _dot_exact_rhs(dt, expand)
        cs_e = _dot_exact_rhs(cs, expand)
        xdt = x * dt_e
        xdt_b = xdt.astype(BF16)
        grow = jnp.exp(cs_e)
        cs_last = cs_e[blk - 1:blk, :]
        to_end = (jnp.exp(cs_last - cs_e) * xdt).astype(BF16)
        carry_decay = jnp.exp(cs_last)
        ys = []
        for g in range(SSM_GROUPS):
            bmat = xbc[:, d_in + g * SSM_D_STATE:d_in + (g + 1) * SSM_D_STATE]
            cmat = xbc[:, d_in + (SSM_GROUPS + g) * SSM_D_STATE:d_in + (SSM_GROUPS + g + 1) * SSM_D_STATE]
            cmat_b = cmat.astype(BF16)
            cb = _dot_nt(cmat_b, bmat.astype(BF16))
            hprev = ht[g]
            y_g = _dot(cmat_b, hprev.astype(BF16)) * grow[:, g * gw:(g + 1) * gw]
            parts = []
            for pr in range(gw // LANES):
                l0 = g * gw + pr * LANES
                slab = xdt_b[:, l0:l0 + LANES]
                hd = l0 // SSM_HEAD_DIM
                res = []
                for hh in (hd, hd + 1):
                    decay = jnp.exp(jnp.where(tril, cs[:, hh:hh + 1] - cs_t[hh:hh + 1, :], -jnp.inf))
                    res.append(_dot((cb * decay).astype(BF16), slab))
                parts.append(jnp.where(lane < SSM_HEAD_DIM, res[0], res[1]))
            y_g = y_g + jnp.concatenate(parts, axis=1)
            ht[g] = carry_decay[:, g * gw:(g + 1) * gw] * hprev + _dot(bmat.T.astype(BF16), to_end[:, g * gw:(g + 1) * gw])
            ys.append(y_g)
        y = jnp.concatenate(ys, axis=1)
        y = (y + dexp_ref[...] * x) * _silu(z)
        outs = []
        for g in range(SSM_GROUPS):
            outs.append(_rms(y[:, g * gw:(g + 1) * gw], gn_ref[:, g * gw:(g + 1) * gw]))
        y = jnp.concatenate(outs, axis=1)
        y_ref[0, pl.ds(t0, rows), :] = y[0:rows].astype(y_ref.dtype)

    def full_block(i, carry):
        block(pl.multiple_of(i * blk, blk), blk)
        return carry

    lax.fori_loop(0, n_full, full_block, 0)
    if tail:
        block(n_full * blk, tail)
    for g in range(SSM_GROUPS):
        for j in range(gw // LANES):
            hfin_ref[0, g * gw + j * LANES:g * gw + (j + 1) * LANES, :] = ht[g, :, j * LANES:(j + 1) * LANES].T


def _ssd(xbc, z, dt, hist8, h0, cw, cb, dtb, alog, dexp, gn):
    b, seq, cdim = xbc.shape
    d_in = SSM_HEADS * SSM_HEAD_DIM
    assert seq % SUBLANES == 0 and (seq % SSD_BLOCK) % BF16_ROWS == 0
    per_b = lambda n: pl.BlockSpec((1, seq, n), lambda i: (i, 0, 0))
    const = lambda shape: pl.BlockSpec(shape, lambda i: (0,) * len(shape))
    return pl.pallas_call(
        functools.partial(_ssd_kernel, seq=seq),
        grid=(b,),
        in_specs=[per_b(cdim), per_b(d_in), per_b(LANES),
                  pl.BlockSpec((1, SUBLANES, cdim), lambda i: (i, 0, 0)),
                  pl.BlockSpec((1, d_in, SSM_D_STATE), lambda i: (i, 0, 0)),
                  const((SSM_CONV, cdim)), const((1, cdim)), const((1, LANES)), const((1, LANES)),
                  const((1, d_in)), const((1, d_in))],
        out_specs=[per_b(d_in), pl.BlockSpec((1, d_in, SSM_D_STATE), lambda i: (i, 0, 0))],
        out_shape=[jax.ShapeDtypeStruct((b, seq, d_in), BF16),
                   jax.ShapeDtypeStruct((b, d_in, SSM_D_STATE), F32)],
        scratch_shapes=[pltpu.VMEM((SUBLANES, cdim), F32),
                        pltpu.VMEM((SSM_GROUPS, SSM_D_STATE, d_in // SSM_GROUPS), F32)],
        compiler_params=_params(("arbitrary",)),
        name="ssd",
    )(xbc, z, dt, hist8, h0, cw, cb, dtb, alog, dexp, gn)


def _out_ffn_kernel(x_ref, da_ref, y_ref, sb_ref, wo_ref, g_ref, wg_ref, wu_ref, wd_ref,
                    fw_ref, fb_ref, init_ref, o_ref, last_ref, act_ref):
    ts = x_ref.shape[1]
    d_ff = wg_ref.shape[1]
    nd = da_ref.shape[2]
    ny = y_ref.shape[2]

    @pl.when(pl.program_id(1) == 0)
    def _():
        last_ref[...] = init_ref[...]

    xr = (x_ref[0] + _dot(da_ref[0], wo_ref[0:nd, :]) + _dot(y_ref[0], wo_ref[nd:nd + ny, :])
          + _dot(sb_ref[0], wo_ref[nd + ny:, :]))
    h2 = _rms(xr, g_ref[...]).astype(BF16)
    row8 = lax.broadcasted_iota(jnp.int32, (SUBLANES, FF_CHUNK), 0)
    for c0 in range(0, d_ff, FF_CHUNK):
        cols = slice(c0, c0 + FF_CHUNK)
        g = _dot(h2, wg_ref[:, cols])
        u = _dot(h2, wu_ref[:, cols])
        hist = last_ref[0, :, cols]
        conv = fw_ref[2:3, cols] * g + fb_ref[:, cols]
        for k in (1, 2):
            shifted = pltpu.roll(g, k, axis=0)
            top = jnp.where(row8 < k, pltpu.roll(hist, k, axis=0), shifted[0:SUBLANES])
            shifted = jnp.concatenate([top, shifted[SUBLANES:]], axis=0)
            conv = conv + fw_ref[2 - k:3 - k, cols] * shifted
        last_ref[0, :, cols] = g[ts - SUBLANES:ts]
        act_ref[:, cols] = (_silu(conv) * u).astype(BF16)
    o_ref[0] = xr + _dot(act_ref[...], wd_ref[...])


def _out_ffn(x, da_o, y, sb_o, wo, g, wg, wu, wd, fw, fb, init8):
    b, seq, d = x.shape
    d_ff = wg.shape[1]
    assert d_ff % FF_CHUNK == 0
    ts = _pick_tile(seq, 704, BF16_ROWS)
    row = lambda n: pl.BlockSpec((1, ts, n), lambda i, j: (i, j, 0))
    state = lambda: pl.BlockSpec((1, SUBLANES, d_ff), lambda i, j: (i, 0, 0))
    return pl.pallas_call(
        _out_ffn_kernel,
        grid=(b, seq // ts),
        in_specs=[row(d), row(da_o.shape[2]), row(y.shape[2]), row(sb_o.shape[2]),
                  _resident(wo.shape), _resident((1, d)), _resident(wg.shape), _resident(wu.shape),
                  _resident(wd.shape), _resident(fw.shape), _resident((1, d_ff)), state()],
        out_specs=[row(d), state()],
        out_shape=[jax.ShapeDtypeStruct((b, seq, d), F32),
                   jax.ShapeDtypeStruct((b, SUBLANES, d_ff), F32)],
        scratch_shapes=[pltpu.VMEM((ts, d_ff), BF16)],
        compiler_params=_params(("arbitrary", "arbitrary")),
        name="out_ffn",
    )(x, da_o, y, sb_o, wo, g, wg, wu, wd, fw, fb, init8)


def _final_kernel(x_ref, g_ref, o_ref, *, skip):
    n = o_ref.shape[1]
    o_ref[0] = _rms(x_ref[0, skip:skip + n, :], g_ref[...])


def _final_norm(x, g, skip):
    b, seq, d = x.shape
    n = seq - skip
    assert skip % SUBLANES == 0
    return pl.pallas_call(
        functools.partial(_final_kernel, skip=skip),
        grid=(b,),
        in_specs=[pl.BlockSpec((1, seq, d), lambda i: (i, 0, 0)), pl.BlockSpec((1, d), lambda i: (0, 0))],
        out_specs=pl.BlockSpec((1, n, d), lambda i: (i, 0, 0)),
        out_shape=jax.ShapeDtypeStruct((b, n, d), F32),
        compiler_params=_params(("arbitrary",)),
        name="final_norm",
    )(x, g)


def _pad_rows_top(a, rows):
    return jnp.pad(a, ((0, 0), (rows - a.shape[1], 0), (0, 0)))


def _layer(x, past, layer, p, lam_init, bias_prompt, bias_sample):
    b, seq, d = x.shape
    proj = _in_proj(x.reshape(b * seq, d), p["norm_mix"], p["w_in"])
    da_q, da_k, da_v, ssm_z, ssm_xbc, sb_q, sb_k, sb_v, ssm_dt = [a.reshape(b, seq, -1) for a in proj]
    if past is None:
        da_o = _da_prompt(da_q, da_k, da_v, bias_prompt, p["lamp"], p["da_subln"], lam_init)
        sb_o = _sb_prompt(sb_q, sb_k, sb_v)
        hist8 = jnp.zeros((b, SUBLANES, ssm_xbc.shape[2]), F32)
        h0 = jnp.zeros((b, SSM_HEADS * SSM_HEAD_DIM, SSM_D_STATE), F32)
        ffn_init = jnp.zeros((b, SUBLANES, p["w_gate"].shape[1]), F32)
    else:
        c_da_k, c_da_v, c_sb_k, c_sb_v, state_ssm, state_conv, state_ffn = past
        da_o = _da_sample(da_q, da_k, da_v, c_da_k, c_da_v, layer, bias_sample, p["lamp"], p["da_subln"], lam_init)
        sb_o = _sb_sample(sb_q, sb_k, sb_v, c_sb_k, c_sb_v, layer)
        hist8 = _pad_rows_top(state_conv[layer], SUBLANES)
        h0 = state_ssm[layer].reshape(b, SSM_HEADS * SSM_HEAD_DIM, SSM_D_STATE)
        ffn_init = _pad_rows_top(state_ffn[layer], SUBLANES)
    y, h_final = _ssd(ssm_xbc, ssm_z, ssm_dt, hist8, h0, p["conv_w"], p["conv_b"], p["dt_bias"],
                      p["a_log"], p["d_exp"], p["ssm_norm"])
    x_new, ffn_last = _out_ffn(x, da_o, y, sb_o, p["w_out"], p["norm_ffn"], p["w_gate"], p["w_up"],
                               p["w_down"], p["fconv_w"], p["fconv_b"], ffn_init)
    conv_in_tail = jnp.concatenate([hist8, ssm_xbc[:, -min(seq, SUBLANES):]], axis=1)
    states = (da_k.reshape(b, seq, DA_HEADS, 2, DA_HEAD_DIM),
              da_v.reshape(b, seq, DA_HEADS, DA_VDIM),
              sb_k.reshape(b, seq, SB_HEADS, SB_HEAD_DIM),
              sb_v.reshape(b, seq, SB_HEADS, SB_HEAD_DIM),
              h_final.reshape(b, SSM_HEADS, SSM_HEAD_DIM, SSM_D_STATE),
              conv_in_tail[:, -(SSM_CONV - 1):],
              ffn_last[:, -(FFN_CONV - 1):])
    return x_new, states


def _pad_lanes(v, n):
    return jnp.pad(v, (0, n - v.shape[0])).reshape(1, n)


def kernel(x_prompt, x_sample, cache_da_k, cache_da_v, cache_sb_k, cache_sb_v, state_ssm, state_ssm_conv, state_ffn_conv, meta_tokens, rel_bias_table, w_in, w_out, norm_mix, norm_ffn, da_lambda_q1, da_lambda_k1, da_lambda_q2, da_lambda_k2, da_subln, ssm_conv_w, ssm_conv_b, ssm_dt_bias, ssm_a_log, ssm_d, ssm_norm, ffn_w_gate, ffn_w_up, ffn_w_down, ffn_conv_w, ffn_conv_b, final_norm):
    depth = w_in.shape[0]
    b = x_prompt.shape[0]
    d = x_prompt.shape[2]
    sb, n_new, _ = x_sample.shape
    past_len = cache_da_k.shape[2]
    n_meta = meta_tokens.shape[0]
    assert n_meta == N_META

    xp = jnp.concatenate([jnp.broadcast_to(meta_tokens[None], (b, n_meta, d)), x_prompt], axis=1)
    xs = x_sample
    bias_prompt, bias_sample = _bias_tiles(rel_bias_table, past_len, n_new)
    cache = (cache_da_k.reshape(depth, sb, past_len, -1), cache_da_v.reshape(depth, sb, past_len, -1),
             cache_sb_k.reshape(depth, sb, past_len, -1), cache_sb_v.reshape(depth, sb, past_len, -1),
             state_ssm, state_ssm_conv, state_ffn_conv)

    p_states, s_states = [], []
    for i in range(depth):
        wi = w_in[i]
        w_r = jnp.concatenate([wi[:, :_OFF_DT], wi[:, _OFF_DT + SSM_HEADS:],
                               jnp.pad(wi[:, _OFF_DT:_OFF_DT + SSM_HEADS], ((0, 0), (0, LANES - SSM_HEADS)))],
                              axis=1).astype(BF16)
        p = dict(
            w_in=w_r, w_out=w_out[i].astype(BF16),
            norm_mix=norm_mix[i].reshape(1, d), norm_ffn=norm_ffn[i].reshape(1, d),
            lamp=jnp.stack([da_lambda_q1[i], da_lambda_k1[i], da_lambda_q2[i], da_lambda_k2[i]]),
            da_subln=da_subln[i].reshape(1, DA_VDIM),
            conv_w=ssm_conv_w[i], conv_b=ssm_conv_b[i].reshape(1, -1),
            dt_bias=_pad_lanes(ssm_dt_bias[i], LANES), a_log=_pad_lanes(ssm_a_log[i], LANES),
            d_exp=jnp.repeat(ssm_d[i], SSM_HEAD_DIM).reshape(1, -1), ssm_norm=ssm_norm[i].reshape(1, -1),
            w_gate=ffn_w_gate[i].astype(BF16), w_up=ffn_w_up[i].astype(BF16), w_down=ffn_w_down[i].astype(BF16),
            fconv_w=ffn_conv_w[i], fconv_b=ffn_conv_b[i].reshape(1, -1),
        )
        lam_init = 0.8 - 0.6 * math.exp(-0.3 * i)
        xp, st_p = _layer(xp, None, i, p, lam_init, bias_prompt, bias_sample)
        xs, st_s = _layer(xs, cache, i, p, lam_init, bias_prompt, bias_sample)
        p_states.append(st_p)
        s_states.append(st_s)

    g_fin = final_norm.reshape(1, d)
    y_prompt = _final_norm(xp, g_fin, n_meta)
    y_sample = _final_norm(xs, g_fin, 0)
    stack = lambda states: tuple(jnp.stack([s[j] for s in states], axis=0) for j in range(7))
    return (y_prompt, y_sample) + stack(p_states) + stack(s_states)
```

```python
import functools
import math

import jax
import jax.numpy as jnp
from jax import lax
from jax.experimental import pallas as pl
from jax.experimental.pallas import tpu as pltpu

F32 = jnp.float32
BF16 = jnp.bfloat16

LANES = 128
SUBLANES = 8
BF16_ROWS = 16
V7X_VMEM_BYTES = 64 * 1024 * 1024
VMEM_LIMIT = 56 * 1024 * 1024

CHUNK = 64
N_META = 16
EPS = 1e-6
NEG_INF = -1e30
DA_HEADS = 4
DA_HEAD_DIM = 64
DA_VDIM = 128
SB_HEADS = 8
SB_HEAD_DIM = 64
SSM_HEADS = 8
SSM_HEAD_DIM = 64
SSM_GROUPS = 2
SSM_D_STATE = 128
SSM_CONV = 4
FFN_CONV = 3
REL_BUCKETS = 32
REL_MAX_DIST = 128
QK_SCALE = 0.125
SSD_BLOCK = 128
FF_CHUNK = 256
KEY_GROUP = 4
GROUP_KEYS = KEY_GROUP * LANES
BIAS_HIDDEN = 4


def _dot(a, b):
    return jnp.dot(a, b, preferred_element_type=F32)


def _dot_nt(a, b):
    return lax.dot_general(a, b, (((1,), (1,)), ((), ())), preferred_element_type=F32)


def _split3(x):
    hi = x.astype(BF16)
    r1 = x - hi.astype(F32)
    mid = r1.astype(BF16)
    lo = (r1 - mid.astype(F32)).astype(BF16)
    return hi, mid, lo


def _dot_exact_rhs(x, m_bf16):
    hi, mid, lo = _split3(x)
    return _dot(hi, m_bf16) + _dot(mid, m_bf16) + _dot(lo, m_bf16)


def _dot_exact_lhs(m_bf16, x):
    hi, mid, lo = _split3(x)
    return _dot(m_bf16, hi) + _dot(m_bf16, mid) + _dot(m_bf16, lo)


def _rms(x, g):
    return x * lax.rsqrt(jnp.mean(x * x, axis=-1, keepdims=True) + EPS) * g


def _silu(x):
    return x * (1.0 / (1.0 + jnp.exp(-x)))


def _softplus(x):
    return jnp.maximum(x, 0.0) + jnp.log1p(jnp.exp(-jnp.abs(x)))


def _pick_tile(n, cap, mult):
    best = None
    for t in range(mult, min(n, cap) + 1, mult):
        if n % t == 0:
            best = t
    assert best is not None, (n, cap, mult)
    return best


def _resident(shape):
    nd = len(shape)
    return pl.BlockSpec(shape, lambda *_: (0,) * nd, pipeline_mode=pl.Buffered(1))


def _params(sem):
    return pltpu.CompilerParams(dimension_semantics=sem, vmem_limit_bytes=VMEM_LIMIT)


def _rel_bucket(rel):
    half = REL_BUCKETS // 2
    max_exact = half // 2
    n = jnp.abs(rel)
    nf = jnp.maximum(n, 1).astype(jnp.float32)
    large = max_exact + (jnp.log(nf / max_exact) / math.log(REL_MAX_DIST / max_exact)
                         * (half - max_exact)).astype(jnp.int32)
    large = jnp.minimum(large, half - 1)
    return jnp.where(rel > 0, half, 0) + jnp.where(n < max_exact, n, large)


def _bias_kernel(tab_ref, bp_ref, bs_ref, op_ref, os_ref, *, past, n_new):
    h = pl.program_id(0)

    def lookup(bucket):
        val = jnp.zeros(bucket.shape, F32)
        for b in range(REL_BUCKETS):
            val = jnp.where(bucket == b, tab_ref[b, h], val)
        return val

    r = lax.broadcasted_iota(jnp.int32, (LANES, LANES), 0)
    c = lax.broadcasted_iota(jnp.int32, (LANES, LANES), 1)
    qc = jnp.right_shift(r - N_META, 6)
    kc = jnp.right_shift(c - N_META, 6)
    for slot in range(BIAS_HIDDEN):
        val = lookup(bp_ref[slot])
        if slot > 0:
            d = slot - 2
            val = jnp.where(2 * d + kc - qc <= 0, val, NEG_INF)
        op_ref[0, slot] = val
    op_ref[0, BIAS_HIDDEN] = jnp.full((LANES, LANES), NEG_INF, F32)

    lw = bs_ref.shape[1]
    rs = lax.broadcasted_iota(jnp.int32, (n_new, lw), 0) + past
    cs = lax.broadcasted_iota(jnp.int32, (n_new, lw), 1)
    vis = (jnp.right_shift(cs, 6) <= jnp.right_shift(rs, 6)) & (cs < past + n_new)
    os_ref[0] = jnp.where(vis, lookup(bs_ref[...]), NEG_INF)


def _bias_tiles(rel_table, past, n_new):
    lw = pl.cdiv(past + n_new, LANES) * LANES
    r = jnp.arange(LANES, dtype=jnp.int32)[:, None]
    c = jnp.arange(LANES, dtype=jnp.int32)[None, :]
    far = jnp.full((LANES, LANES), -2 * LANES, jnp.int32)
    bp = jnp.stack([_rel_bucket(far)] + [_rel_bucket(d * LANES + c - r) for d in (-1, 0, 1)])
    qs = past + jnp.arange(n_new, dtype=jnp.int32)[:, None]
    ks = jnp.arange(lw, dtype=jnp.int32)[None, :]
    bs = _rel_bucket(ks - qs)
    return pl.pallas_call(
        functools.partial(_bias_kernel, past=past, n_new=n_new),
        grid=(DA_HEADS,),
        in_specs=[pl.BlockSpec(memory_space=pltpu.SMEM),
                  pl.BlockSpec((4, LANES, LANES), lambda h: (0, 0, 0)),
                  pl.BlockSpec((n_new, lw), lambda h: (0, 0))],
        out_specs=[pl.BlockSpec((1, BIAS_HIDDEN + 1, LANES, LANES), lambda h: (h, 0, 0, 0)),
                   pl.BlockSpec((1, n_new, lw), lambda h: (h, 0, 0))],
        out_shape=[jax.ShapeDtypeStruct((DA_HEADS, BIAS_HIDDEN + 1, LANES, LANES), F32),
                   jax.ShapeDtypeStruct((DA_HEADS, n_new, lw), F32)],
        name="rel_bias_tiles",
    )(rel_table, bp, bs)


_SEG = (("da_q", 512, BF16), ("da_k", 512, F32), ("da_v", 512, F32), ("ssm_z", 512, F32),
        ("ssm_xbc", 1024, F32), ("sb_q", 512, BF16), ("sb_k", 512, F32), ("sb_v", 512, F32),
        ("ssm_dt", LANES, F32))
_N_PROJ = sum(s[1] for s in _SEG)
_OFF_DT = 3072


def _inproj_kernel(x_ref, g_ref, w_ref, *out_refs):
    h = _rms(x_ref[...], g_ref[...]).astype(BF16)
    off = 0
    for (_, n, dt), o_ref in zip(_SEG, out_refs):
        o_ref[...] = _dot(h, w_ref[:, off:off + n]).astype(dt)
        off += n


def _in_proj(x2d, g, w_r):
    t, d = x2d.shape
    tm = _pick_tile(t, 256, BF16_ROWS)
    return pl.pallas_call(
        _inproj_kernel,
        grid=(t // tm,),
        in_specs=[pl.BlockSpec((tm, d), lambda i: (i, 0)), _resident((1, d)), _resident((d, _N_PROJ))],
        out_specs=[pl.BlockSpec((tm, n), lambda i: (i, 0)) for _, n, _ in _SEG],
        out_shape=[jax.ShapeDtypeStruct((t, n), dt) for _, n, dt in _SEG],
        compiler_params=_params(("arbitrary",)),
        name="in_proj",
    )(x2d, g, w_r)


def _lambda(lamp_ref, lam_init):
    lp = lamp_ref[...]
    return (jnp.exp(jnp.sum(lp[0:1] * lp[1:2], axis=1, keepdims=True))
            - jnp.exp(jnp.sum(lp[2:3] * lp[3:4], axis=1, keepdims=True)) + lam_init)


def _split_maps(q_bf16):
    lane = lax.broadcasted_iota(jnp.int32, (1, LANES), 1)
    qf = q_bf16.astype(F32) * QK_SCALE
    return (jnp.where(lane < DA_HEAD_DIM, qf, 0.0).astype(BF16),
            jnp.where(lane >= DA_HEAD_DIM, qf, 0.0).astype(BF16))


def _da_prompt_kernel(q_ref, k_ref, v_ref, bm_ref, lamp_ref, subln_ref, o_ref,
                      q1s, q2s, ks, vs, sbuf, *, seq, lam_init):
    n_full, tail = divmod(seq, LANES)
    n_blk = n_full + (1 if tail else 0)
    lp = ks.shape[0]
    q1, q2 = _split_maps(q_ref[0])
    q1s[...] = q1
    q2s[...] = q2
    ks[0:seq, :] = k_ref[0].astype(BF16)
    vs[0:seq, :] = v_ref[0].astype(BF16)
    ks[seq:lp, :] = jnp.zeros((lp - seq, LANES), BF16)
    vs[seq:lp, :] = jnp.zeros((lp - seq, LANES), BF16)
    lam = _lambda(lamp_ref, lam_init)
    gsub = subln_ref[...] * (1.0 - lam_init)

    def tile(i, rows, r0):
        q1t = q1s[pl.ds(r0, rows), :]
        q2t = q2s[pl.ds(r0, rows), :]
        n_grp = (jnp.minimum(i + 2, n_blk) + KEY_GROUP - 1) // KEY_GROUP

        def scores(grp, carry):
            m1, m2 = carry
            base = pl.multiple_of(grp * GROUP_KEYS, GROUP_KEYS)
            for g in range(KEY_GROUP):
                kb = grp * KEY_GROUP + g
                kblk = ks[pl.ds(base + g * LANES, LANES), :]
                slot = jnp.where(kb >= n_blk, BIAS_HIDDEN, jnp.clip(kb - i + 2, 0, BIAS_HIDDEN))
                bias = bm_ref[0, slot, 0:rows, :]
                s1 = _dot_nt(q1t, kblk) + bias
                s2 = _dot_nt(q2t, kblk) + bias
                sbuf[0, grp, 0:rows, g * LANES:(g + 1) * LANES] = s1
                sbuf[1, grp, 0:rows, g * LANES:(g + 1) * LANES] = s2
                m1 = jnp.maximum(m1, s1)
                m2 = jnp.maximum(m2, s2)
            return m1, m2

        ninf = jnp.full((rows, LANES), -jnp.inf, F32)
        m1, m2 = lax.fori_loop(0, n_grp, scores, (ninf, ninf))
        m1 = jnp.max(m1, axis=1, keepdims=True)
        m2 = jnp.max(m2, axis=1, keepdims=True)

        def exp_pv(grp, carry):
            l1, l2, a1, a2 = carry
            base = pl.multiple_of(grp * GROUP_KEYS, GROUP_KEYS)
            ps1, ps2 = [], []
            for g in range(KEY_GROUP):
                p1 = jnp.exp(sbuf[0, grp, 0:rows, g * LANES:(g + 1) * LANES] - m1)
                p2 = jnp.exp(sbuf[1, grp, 0:rows, g * LANES:(g + 1) * LANES] - m2)
                l1 = l1 + p1
                l2 = l2 + p2
                ps1.append(p1.astype(BF16))
                ps2.append(p2.astype(BF16))
            vrows = vs[pl.ds(base, GROUP_KEYS), :]
            a1 = a1 + _dot(jnp.concatenate(ps1, axis=1), vrows)
            a2 = a2 + _dot(jnp.concatenate(ps2, axis=1), vrows)
            return l1, l2, a1, a2

        zero = jnp.zeros((rows, LANES), F32)
        l1, l2, a1, a2 = lax.fori_loop(0, n_grp, exp_pv, (zero, zero, zero, zero))
        c1 = 1.0 / jnp.sum(l1, axis=1, keepdims=True)
        c2 = lam / jnp.sum(l2, axis=1, keepdims=True)
        o_ref[0, pl.ds(r0, rows), :] = _rms(a1 * c1 - a2 * c2, gsub).astype(o_ref.dtype)

    def full_tile(i, carry):
        tile(i, LANES, pl.multiple_of(i * LANES, LANES))
        return carry

    lax.fori_loop(0, n_full, full_tile, 0)
    if tail:
        tile(n_full, tail, n_full * LANES)


def _da_prompt(q, k, v, bm, lamp, subln, lam_init):
    b, seq, _ = q.shape
    assert seq % BF16_ROWS == 0 and (seq - N_META) % CHUNK == 0
    n_grp = pl.cdiv(pl.cdiv(seq, LANES), KEY_GROUP)
    lp = n_grp * GROUP_KEYS
    blk = lambda: pl.BlockSpec((1, seq, LANES), lambda i, h: (i, 0, h))
    return pl.pallas_call(
        functools.partial(_da_prompt_kernel, seq=seq, lam_init=lam_init),
        grid=(b, DA_HEADS),
        in_specs=[blk(), blk(), blk(),
                  pl.BlockSpec((1, BIAS_HIDDEN + 1, LANES, LANES), lambda i, h: (h, 0, 0, 0)),
                  pl.BlockSpec((4, DA_HEAD_DIM), lambda i, h: (0, 0)),
                  pl.BlockSpec((1, DA_VDIM), lambda i, h: (0, 0))],
        out_specs=blk(),
        out_shape=jax.ShapeDtypeStruct((b, seq, DA_HEADS * DA_VDIM), BF16),
        scratch_shapes=[pltpu.VMEM((seq, LANES), BF16), pltpu.VMEM((seq, LANES), BF16),
                        pltpu.VMEM((lp, LANES), BF16), pltpu.VMEM((lp, LANES), BF16),
                        pltpu.VMEM((2, n_grp, LANES, GROUP_KEYS), F32)],
        compiler_params=_params(("arbitrary", "arbitrary")),
        name="diff_attn_prompt",
    )(q, k, v, bm, lamp, subln)


def _da_sample_kernel(q_ref, k_ref, v_ref, ck_ref, cv_ref, bs_ref, lamp_ref, subln_ref, o_ref,
                      ks, vs, *, past, n_new, lam_init):
    lw = ks.shape[0]
    tot = past + n_new
    ks[0:past, :] = ck_ref[0, 0].astype(BF16)
    vs[0:past, :] = cv_ref[0, 0].astype(BF16)
    ks[past:tot, :] = k_ref[0].astype(BF16)
    vs[past:tot, :] = v_ref[0].astype(BF16)
    if lw > tot:
        ks[tot:lw, :] = jnp.zeros((lw - tot, LANES), BF16)
        vs[tot:lw, :] = jnp.zeros((lw - tot, LANES), BF16)
    q1, q2 = _split_maps(q_ref[0])
    bias = bs_ref[0]
    kall = ks[...]
    s1 = _dot_nt(q1, kall) + bias
    s2 = _dot_nt(q2, kall) + bias
    p1 = jnp.exp(s1 - jnp.max(s1, axis=1, keepdims=True))
    p2 = jnp.exp(s2 - jnp.max(s2, axis=1, keepdims=True))
    c1 = 1.0 / jnp.sum(p1, axis=1, keepdims=True)
    c2 = _lambda(lamp_ref, lam_init) / jnp.sum(p2, axis=1, keepdims=True)
    acc = _dot((p1 * c1 - p2 * c2).astype(BF16), vs[...])
    o_ref[0] = _rms(acc, subln_ref[...] * (1.0 - lam_init)).astype(o_ref.dtype)


def _da_sample(q, k, v, cache_k, cache_v, layer, bs, lamp, subln, lam_init):
    b, n_new, _ = q.shape
    past = cache_k.shape[2]
    assert past % BF16_ROWS == 0 and n_new % BF16_ROWS == 0
    lw = bs.shape[2]
    new = lambda: pl.BlockSpec((1, n_new, LANES), lambda i, h: (i, 0, h))
    old = lambda: pl.BlockSpec((1, 1, past, LANES), lambda i, h: (layer, i, 0, h))
    return pl.pallas_call(
        functools.partial(_da_sample_kernel, past=past, n_new=n_new, lam_init=lam_init),
        grid=(b, DA_HEADS),
        in_specs=[new(), new(), new(), old(), old(),
                  pl.BlockSpec((1, n_new, lw), lambda i, h: (h, 0, 0)),
                  pl.BlockSpec((4, DA_HEAD_DIM), lambda i, h: (0, 0)),
                  pl.BlockSpec((1, DA_VDIM), lambda i, h: (0, 0))],
        out_specs=new(),
        out_shape=jax.ShapeDtypeStruct((b, n_new, DA_HEADS * DA_VDIM), BF16),
        scratch_shapes=[pltpu.VMEM((lw, LANES), BF16), pltpu.VMEM((lw, LANES), BF16)],
        compiler_params=_params(("arbitrary", "arbitrary")),
        name="diff_attn_sample",
    )(q, k, v, cache_k, cache_v, bs, lamp, subln)


def _split_heads(q_bf16):
    lane = lax.broadcasted_iota(jnp.int32, (1, LANES), 1)
    qf = q_bf16.astype(F32) * QK_SCALE
    return (jnp.where(lane < SB_HEAD_DIM, qf, 0.0).astype(BF16),
            jnp.where(lane >= SB_HEAD_DIM, qf, 0.0).astype(BF16))


def _sb_tile(i, rows, qts, ks, vs, suffix):
    row = lax.broadcasted_iota(jnp.int32, (rows, LANES), 0)
    col = lax.broadcasted_iota(jnp.int32, (rows, LANES), 1)
    col_minus_row = col - row
    top = i // KEY_GROUP

    def group(grp, runs, accs, diagonal):
        base = pl.multiple_of(grp * GROUP_KEYS, GROUP_KEYS)
        vrows = vs[pl.ds(base, GROUP_KEYS), :]
        new_runs, new_accs = [], []
        for qt, run, acc in zip(qts, runs, accs):
            log_betas, inners, totals, visible = [], [], [], []
            for g in range(KEY_GROUP):
                z = _dot_nt(qt, ks[pl.ds(base + g * LANES, LANES), :])
                log_beta = jnp.minimum(z, 0.0) - jnp.log(1.0 + jnp.exp(-jnp.abs(z)))
                log_keep = log_beta - z
                if diagonal:
                    vis = col_minus_row < (i - grp * KEY_GROUP - g) * LANES
                    log_keep = jnp.where(vis, log_keep, 0.0)
                    visible.append(vis)
                hi = log_keep.astype(BF16)
                lo = (log_keep - hi.astype(F32)).astype(BF16)
                both = _dot(jnp.concatenate([hi, lo], axis=1), suffix)
                log_betas.append(log_beta)
                inners.append(both[:, 0:LANES])
                totals.append(both[:, LANES:])
            ws = [None] * KEY_GROUP
            for g in reversed(range(KEY_GROUP)):
                w = jnp.exp(log_betas[g] + inners[g] + run)
                if diagonal:
                    w = jnp.where(visible[g], w, 0.0)
                ws[g] = w.astype(BF16)
                run = run + totals[g]
            new_runs.append(run)
            new_accs.append(acc + _dot(jnp.concatenate(ws, axis=1), vrows))
        return new_runs, new_accs

    zero = jnp.zeros((rows, LANES), F32)
    runs, accs = group(top, [zero, zero], [zero, zero], True)

    def body(t, carry):
        r, a = group(top - 1 - t, carry[0:2], carry[2:4], False)
        return (r[0], r[1], a[0], a[1])

    carry = lax.fori_loop(0, top, body, (runs[0], runs[1], accs[0], accs[1]))
    lane = lax.broadcasted_iota(jnp.int32, (1, LANES), 1)
    return jnp.where(lane < SB_HEAD_DIM, carry[2], carry[3])


def _suffix_matrix():
    r = lax.broadcasted_iota(jnp.int32, (2 * LANES, 2 * LANES), 0)
    c = lax.broadcasted_iota(jnp.int32, (2 * LANES, 2 * LANES), 1)
    later = jnp.bitwise_and(r, LANES - 1) > c
    return jnp.where((c >= LANES) | later, 1.0, 0.0).astype(BF16)


def _sb_prompt_kernel(q_ref, k_ref, v_ref, o_ref, qas, qbs, ks, vs, *, seq):
    n_full, tail = divmod(seq, LANES)
    lp = ks.shape[0]
    qa, qb = _split_heads(q_ref[0])
    qas[...] = qa
    qbs[...] = qb
    ks[0:seq, :] = k_ref[0].astype(BF16)
    vs[0:seq, :] = v_ref[0].astype(BF16)
    if lp > seq:
        ks[seq:lp, :] = jnp.zeros((lp - seq, LANES), BF16)
        vs[seq:lp, :] = jnp.zeros((lp - seq, LANES), BF16)
    suffix = _suffix_matrix()

    def full_tile(i, carry):
        r0 = pl.multiple_of(i * LANES, LANES)
        o = _sb_tile(i, LANES, (qas[pl.ds(r0, LANES), :], qbs[pl.ds(r0, LANES), :]), ks, vs, suffix)
        o_ref[0, pl.ds(r0, LANES), :] = o.astype(o_ref.dtype)
        return carry

    lax.fori_loop(0, n_full, full_tile, 0)
    if tail:
        r0 = n_full * LANES
        o = _sb_tile(n_full, tail, (qas[r0:r0 + tail, :], qbs[r0:r0 + tail, :]), ks, vs, suffix)
        o_ref[0, r0:r0 + tail, :] = o.astype(o_ref.dtype)


def _sb_prompt(q, k, v):
    b, seq, _ = q.shape
    assert seq % BF16_ROWS == 0
    lp = pl.cdiv(pl.cdiv(seq, LANES), KEY_GROUP) * GROUP_KEYS
    blk = lambda: pl.BlockSpec((1, seq, LANES), lambda i, p: (i, 0, p))
    return pl.pallas_call(
        functools.partial(_sb_prompt_kernel, seq=seq),
        grid=(b, SB_HEADS // 2),
        in_specs=[blk(), blk(), blk()],
        out_specs=blk(),
        out_shape=jax.ShapeDtypeStruct((b, seq, SB_HEADS * SB_HEAD_DIM), BF16),
        scratch_shapes=[pltpu.VMEM((seq, LANES), BF16), pltpu.VMEM((seq, LANES), BF16),
                        pltpu.VMEM((lp, LANES), BF16), pltpu.VMEM((lp, LANES), BF16)],
        compiler_params=_params(("arbitrary", "arbitrary")),
        name="stick_prompt",
    )(q, k, v)


def _sb_sample_kernel(q_ref, k_ref, v_ref, ck_ref, cv_ref, o_ref, ks, vs, *, past, n_new):
    lw = ks.shape[0]
    tot = past + n_new
    ks[0:past, :] = ck_ref[0, 0].astype(BF16)
    vs[0:past, :] = cv_ref[0, 0].astype(BF16)
    ks[past:tot, :] = k_ref[0].astype(BF16)
    vs[past:tot, :] = v_ref[0].astype(BF16)
    if lw > tot:
        ks[tot:lw, :] = jnp.zeros((lw - tot, LANES), BF16)
        vs[tot:lw, :] = jnp.zeros((lw - tot, LANES), BF16)
    o = _sb_tile(past // LANES, n_new, _split_heads(q_ref[0]), ks, vs, _suffix_matrix())
    o_ref[0] = o.astype(o_ref.dtype)


def _sb_sample(q, k, v, cache_k, cache_v, layer):
    b, n_new, _ = q.shape
    past = cache_k.shape[2]
    assert past % LANES == 0 and n_new <= LANES and n_new % BF16_ROWS == 0
    lw = (past // GROUP_KEYS + 1) * GROUP_KEYS
    new = lambda: pl.BlockSpec((1, n_new, LANES), lambda i, p: (i, 0, p))
    old = lambda: pl.BlockSpec((1, 1, past, LANES), lambda i, p: (layer, i, 0, p))
    return pl.pallas_call(
        functools.partial(_sb_sample_kernel, past=past, n_new=n_new),
        grid=(b, SB_HEADS // 2),
        in_specs=[new(), new(), new(), old(), old()],
        out_specs=new(),
        out_shape=jax.ShapeDtypeStruct((b, n_new, SB_HEADS * SB_HEAD_DIM), BF16),
        scratch_shapes=[pltpu.VMEM((lw, LANES), BF16), pltpu.VMEM((lw, LANES), BF16)],
        compiler_params=_params(("arbitrary", "arbitrary")),
        name="stick_sample",
    )(q, k, v, cache_k, cache_v)


def _ssd_kernel(xbc_ref, z_ref, dt_ref, hist_ref, h0_ref, cw_ref, cb_ref, dtb_ref, alog_ref,
                dexp_ref, gn_ref, y_ref, hfin_ref, prev8, ht, *, seq):
    d_in = SSM_HEADS * SSM_HEAD_DIM
    gw = d_in // SSM_GROUPS
    n_full, tail = divmod(seq, SSD_BLOCK)
    blk = SSD_BLOCK

    r = lax.broadcasted_iota(jnp.int32, (blk, blk), 0)
    c = lax.broadcasted_iota(jnp.int32, (blk, blk), 1)
    tril = r >= c
    tri = jnp.where(tril, 1.0, 0.0).astype(BF16)
    er = lax.broadcasted_iota(jnp.int32, (LANES, d_in), 0)
    ec = lax.broadcasted_iota(jnp.int32, (LANES, d_in), 1)
    expand = jnp.where(jnp.right_shift(ec, 6) == er, 1.0, 0.0).astype(BF16)
    lane = lax.broadcasted_iota(jnp.int32, (1, LANES), 1)

    prev8[...] = hist_ref[0]
    for g in range(SSM_GROUPS):
        for j in range(gw // LANES):
            rows = h0_ref[0, g * gw + j * LANES:g * gw + (j + 1) * LANES, :]
            ht[g, :, j * LANES:(j + 1) * LANES] = rows.T

    a_neg = -jnp.exp(alog_ref[...])
    cw = cw_ref[...]

    def block(t0, rows):
        xin = xbc_ref[0, pl.ds(t0, rows), :]
        win = jnp.concatenate([prev8[...], xin], axis=0)
        conv = cw[3:4] * win
        for k in range(1, SSM_CONV):
            conv = conv + cw[3 - k:4 - k] * pltpu.roll(win, k, axis=0)
        xbc = _silu(conv[SUBLANES:] + cb_ref[...])
        prev8[...] = xin[rows - SUBLANES:rows]
        dt = _softplus(dt_ref[0, pl.ds(t0, rows), :] + dtb_ref[...])
        z = z_ref[0, pl.ds(t0, rows), :]
        if rows < blk:
            pad = blk - rows
            xbc = jnp.concatenate([xbc, jnp.zeros((pad, xbc.shape[1]), F32)], axis=0)
            dt = jnp.concatenate([dt, jnp.zeros((pad, LANES), F32)], axis=0)
            z = jnp.concatenate([z, jnp.zeros((pad, d_in), F32)], axis=0)
        x = xbc[:, 0:d_in]
        cs = _dot_exact_lhs(tri, dt * a_neg)
        cs_t = cs.T
        dt_e = _dot_exact_rhs(dt, expand)
        cs_e = _dot_exact_rhs(cs, expand)
        xdt = x * dt_e
        xdt_b = xdt.astype(BF16)
        grow = jnp.exp(cs_e)
        cs_last = cs_e[blk - 1:blk, :]
        to_end = (jnp.exp(cs_last - cs_e) * xdt).astype(BF16)
        carry_decay = jnp.exp(cs_last)
        ys = []
        for g in range(SSM_GROUPS):
            bmat = xbc[:, d_in + g * SSM_D_STATE:d_in + (g + 1) * SSM_D_STATE]
            cmat = xbc[:, d_in + (SSM_GROUPS + g) * SSM_D_STATE:d_in + (SSM_GROUPS + g + 1) * SSM_D_STATE]
            cmat_b = cmat.astype(BF16)
            cb = _dot_nt(cmat_b, bmat.astype(BF16))
            hprev = ht[g]
            y_g = _dot(cmat_b, hprev.astype(BF16)) * grow[:, g * gw:(g + 1) * gw]
            parts = []
            for pr in range(gw // LANES):
                l0 = g * gw + pr * LANES
                slab = xdt_b[:, l0:l0 + LANES]
                hd = l0 // SSM_HEAD_DIM
                res = []
                for hh in (hd, hd + 1):
                    decay = jnp.exp(jnp.where(tril, cs[:, hh:hh + 1] - cs_t[hh:hh + 1, :], -jnp.inf))
                    res.append(_dot((cb * decay).astype(BF16), slab))
                parts.append(jnp.where(lane < SSM_HEAD_DIM, res[0], res[1]))
            y_g = y_g + jnp.concatenate(parts, axis=1)
            ht[g] = carry_decay[:, g * gw:(g + 1) * gw] * hprev + _dot(bmat.T.astype(BF16), to_end[:, g * gw:(g + 1) * gw])
            ys.append(y_g)
        y = jnp.concatenate(ys, axis=1)
        y = (y + dexp_ref[...] * x) * _silu(z)
        outs = []
        for g in range(SSM_GROUPS):
            outs.append(_rms(y[:, g * gw:(g + 1) * gw], gn_ref[:, g * gw:(g + 1) * gw]))
        y = jnp.concatenate(outs, axis=1)
        y_ref[0, pl.ds(t0, rows), :] = y[0:rows].astype(y_ref.dtype)

    def full_block(i, carry):
        block(pl.multiple_of(i * blk, blk), blk)
        return carry

    lax.fori_loop(0, n_full, full_block, 0)
    if tail:
        block(n_full * blk, tail)
    for g in range(SSM_GROUPS):
        for j in range(gw // LANES):
            hfin_ref[0, g * gw + j * LANES:g * gw + (j + 1) * LANES, :] = ht[g, :, j * LANES:(j + 1) * LANES].T


def _ssd(xbc, z, dt, hist8, h0, cw, cb, dtb, alog, dexp, gn):
    b, seq, cdim = xbc.shape
    d_in = SSM_HEADS * SSM_HEAD_DIM
    assert seq % SUBLANES == 0 and (seq % SSD_BLOCK) % BF16_ROWS == 0
    per_b = lambda n: pl.BlockSpec((1, seq, n), lambda i: (i, 0, 0))
    const = lambda shape: pl.BlockSpec(shape, lambda i: (0,) * len(shape))
    return pl.pallas_call(
        functools.partial(_ssd_kernel, seq=seq),
        grid=(b,),
        in_specs=[per_b(cdim), per_b(d_in), per_b(LANES),
                  pl.BlockSpec((1, SUBLANES, cdim), lambda i: (i, 0, 0)),
                  pl.BlockSpec((1, d_in, SSM_D_STATE), lambda i: (i, 0, 0)),
                  const((SSM_CONV, cdim)), const((1, cdim)), const((1, LANES)), const((1, LANES)),
                  const((1, d_in)), const((1, d_in))],
        out_specs=[per_b(d_in), pl.BlockSpec((1, d_in, SSM_D_STATE), lambda i: (i, 0, 0))],
        out_shape=[jax.ShapeDtypeStruct((b, seq, d_in), BF16),
                   jax.ShapeDtypeStruct((b, d_in, SSM_D_STATE), F32)],
        scratch_shapes=[pltpu.VMEM((SUBLANES, cdim), F32),
                        pltpu.VMEM((SSM_GROUPS, SSM_D_STATE, d_in // SSM_GROUPS), F32)],
        compiler_params=_params(("arbitrary",)),
        name="ssd",
    )(xbc, z, dt, hist8, h0, cw, cb, dtb, alog, dexp, gn)


def _out_ffn_kernel(x_ref, da_ref, y_ref, sb_ref, wo_ref, g_ref, wg_ref, wu_ref, wd_ref,
                    fw_ref, fb_ref, init_ref, o_ref, last_ref, act_ref):
    ts = x_ref.shape[1]
    d_ff = wg_ref.shape[1]
    nd = da_ref.shape[2]
    ny = y_ref.shape[2]

    @pl.when(pl.program_id(1) == 0)
    def _():
        last_ref[...] = init_ref[...]

    xr = (x_ref[0] + _dot(da_ref[0], wo_ref[0:nd, :]) + _dot(y_ref[0], wo_ref[nd:nd + ny, :])
          + _dot(sb_ref[0], wo_ref[nd + ny:, :]))
    h2 = _rms(xr, g_ref[...]).astype(BF16)
    row8 = lax.broadcasted_iota(jnp.int32, (SUBLANES, FF_CHUNK), 0)
    for c0 in range(0, d_ff, FF_CHUNK):
        cols = slice(c0, c0 + FF_CHUNK)
        g = _dot(h2, wg_ref[:, cols])
        u = _dot(h2, wu_ref[:, cols])
        hist = last_ref[0, :, cols]
        conv = fw_ref[2:3, cols] * g + fb_ref[:, cols]
        for k in (1, 2):
            shifted = pltpu.roll(g, k, axis=0)
            top = jnp.where(row8 < k, pltpu.roll(hist, k, axis=0), shifted[0:SUBLANES])
            shifted = jnp.concatenate([top, shifted[SUBLANES:]], axis=0)
            conv = conv + fw_ref[2 - k:3 - k, cols] * shifted
        last_ref[0, :, cols] = g[ts - SUBLANES:ts]
        act_ref[:, cols] = (_silu(conv) * u).astype(BF16)
    o_ref[0] = xr + _dot(act_ref[...], wd_ref[...])


def _out_ffn(x, da_o, y, sb_o, wo, g, wg, wu, wd, fw, fb, init8):
    b, seq, d = x.shape
    d_ff = wg.shape[1]
    assert d_ff % FF_CHUNK == 0
    ts = _pick_tile(seq, 704, BF16_ROWS)
    row = lambda n: pl.BlockSpec((1, ts, n), lambda i, j: (i, j, 0))
    state = lambda: pl.BlockSpec((1, SUBLANES, d_ff), lambda i, j: (i, 0, 0))
    return pl.pallas_call(
        _out_ffn_kernel,
        grid=(b, seq // ts),
        in_specs=[row(d), row(da_o.shape[2]), row(y.shape[2]), row(sb_o.shape[2]),
                  _resident(wo.shape), _resident((1, d)), _resident(wg.shape), _resident(wu.shape),
                  _resident(wd.shape), _resident(fw.shape), _resident((1, d_ff)), state()],
        out_specs=[row(d), state()],
        out_shape=[jax.ShapeDtypeStruct((b, seq, d), F32),
                   jax.ShapeDtypeStruct((b, SUBLANES, d_ff), F32)],
        scratch_shapes=[pltpu.VMEM((ts, d_ff), BF16)],
        compiler_params=_params(("arbitrary", "arbitrary")),
        name="out_ffn",
    )(x, da_o, y, sb_o, wo, g, wg, wu, wd, fw, fb, init8)


def _final_kernel(x_ref, g_ref, o_ref, *, skip):
    n = o_ref.shape[1]
    o_ref[0] = _rms(x_ref[0, skip:skip + n, :], g_ref[...])


def _final_norm(x, g, skip):
    b, seq, d = x.shape
    n = seq - skip
    assert skip % SUBLANES == 0
    return pl.pallas_call(
        functools.partial(_final_kernel, skip=skip),
        grid=(b,),
        in_specs=[pl.BlockSpec((1, seq, d), lambda i: (i, 0, 0)), pl.BlockSpec((1, d), lambda i: (0, 0))],
        out_specs=pl.BlockSpec((1, n, d), lambda i: (i, 0, 0)),
        out_shape=jax.ShapeDtypeStruct((b, n, d), F32),
        compiler_params=_params(("arbitrary",)),
        name="final_norm",
    )(x, g)


def _pad_rows_top(a, rows):
    return jnp.pad(a, ((0, 0), (rows - a.shape[1], 0), (0, 0)))


def _layer(x, past, layer, p, lam_init, bias_prompt, bias_sample):
    b, seq, d = x.shape
    proj = _in_proj(x.reshape(b * seq, d), p["norm_mix"], p["w_in"])
    da_q, da_k, da_v, ssm_z, ssm_xbc, sb_q, sb_k, sb_v, ssm_dt = [a.reshape(b, seq, -1) for a in proj]
    if past is None:
        da_o = _da_prompt(da_q, da_k, da_v, bias_prompt, p["lamp"], p["da_subln"], lam_init)
        sb_o = _sb_prompt(sb_q, sb_k, sb_v)
        hist8 = jnp.zeros((b, SUBLANES, ssm_xbc.shape[2]), F32)
        h0 = jnp.zeros((b, SSM_HEADS * SSM_HEAD_DIM, SSM_D_STATE), F32)
        ffn_init = jnp.zeros((b, SUBLANES, p["w_gate"].shape[1]), F32)
    else:
        c_da_k, c_da_v, c_sb_k, c_sb_v, state_ssm, state_conv, state_ffn = past
        da_o = _da_sample(da_q, da_k, da_v, c_da_k, c_da_v, layer, bias_sample, p["lamp"], p["da_subln"], lam_init)
        sb_o = _sb_sample(sb_q, sb_k, sb_v, c_sb_k, c_sb_v, layer)
        hist8 = _pad_rows_top(state_conv[layer], SUBLANES)
        h0 = state_ssm[layer].reshape(b, SSM_HEADS * SSM_HEAD_DIM, SSM_D_STATE)
        ffn_init = _pad_rows_top(state_ffn[layer], SUBLANES)
    y, h_final = _ssd(ssm_xbc, ssm_z, ssm_dt, hist8, h0, p["conv_w"], p["conv_b"], p["dt_bias"],
                      p["a_log"], p["d_exp"], p["ssm_norm"])
    x_new, ffn_last = _out_ffn(x, da_o, y, sb_o, p["w_out"], p["norm_ffn"], p["w_gate"], p["w_up"],
                               p["w_down"], p["fconv_w"], p["fconv_b"], ffn_init)
    conv_in_tail = jnp.concatenate([hist8, ssm_xbc[:, -min(seq, SUBLANES):]], axis=1)
    states = (da_k.reshape(b, seq, DA_HEADS, 2, DA_HEAD_DIM),
              da_v.reshape(b, seq, DA_HEADS, DA_VDIM),
              sb_k.reshape(b, seq, SB_HEADS, SB_HEAD_DIM),
              sb_v.reshape(b, seq, SB_HEADS, SB_HEAD_DIM),
              h_final.reshape(b, SSM_HEADS, SSM_HEAD_DIM, SSM_D_STATE),
              conv_in_tail[:, -(SSM_CONV - 1):],
              ffn_last[:, -(FFN_CONV - 1):])
    return x_new, states


def _pad_lanes(v, n):
    return jnp.pad(v, (0, n - v.shape[0])).reshape(1, n)


def kernel(x_prompt, x_sample, cache_da_k, cache_da_v, cache_sb_k, cache_sb_v, state_ssm, state_ssm_conv, state_ffn_conv, meta_tokens, rel_bias_table, w_in, w_out, norm_mix, norm_ffn, da_lambda_q1, da_lambda_k1, da_lambda_q2, da_lambda_k2, da_subln, ssm_conv_w, ssm_conv_b, ssm_dt_bias, ssm_a_log, ssm_d, ssm_norm, ffn_w_gate, ffn_w_up, ffn_w_down, ffn_conv_w, ffn_conv_b, final_norm):
    depth = w_in.shape[0]
    b = x_prompt.shape[0]
    d = x_prompt.shape[2]
    sb, n_new, _ = x_sample.shape
    past_len = cache_da_k.shape[2]
    n_meta = meta_tokens.shape[0]
    assert n_meta == N_META

    xp = jnp.concatenate([jnp.broadcast_to(meta_tokens[None], (b, n_meta, d)), x_prompt], axis=1)
    xs = x_sample
    bias_prompt, bias_sample = _bias_tiles(rel_bias_table, past_len, n_new)
    cache = (cache_da_k.reshape(depth, sb, past_len, -1), cache_da_v.reshape(depth, sb, past_len, -1),
             cache_sb_k.reshape(depth, sb, past_len, -1), cache_sb_v.reshape(depth, sb, past_len, -1),
             state_ssm, state_ssm_conv, state_ffn_conv)

    p_states, s_states = [], []
    for i in range(depth):
        wi = w_in[i]
        w_r = jnp.concatenate([wi[:, :_OFF_DT], wi[:, _OFF_DT + SSM_HEADS:],
                               jnp.pad(wi[:, _OFF_DT:_OFF_DT + SSM_HEADS], ((0, 0), (0, LANES - SSM_HEADS)))],
                              axis=1).astype(BF16)
        p = dict(
            w_in=w_r, w_out=w_out[i].astype(BF16),
            norm_mix=norm_mix[i].reshape(1, d), norm_ffn=norm_ffn[i].reshape(1, d),
            lamp=jnp.stack([da_lambda_q1[i], da_lambda_k1[i], da_lambda_q2[i], da_lambda_k2[i]]),
            da_subln=da_subln[i].reshape(1, DA_VDIM),
            conv_w=ssm_conv_w[i], conv_b=ssm_conv_b[i].reshape(1, -1),
            dt_bias=_pad_lanes(ssm_dt_bias[i], LANES), a_log=_pad_lanes(ssm_a_log[i], LANES),
            d_exp=jnp.repeat(ssm_d[i], SSM_HEAD_DIM).reshape(1, -1), ssm_norm=ssm_norm[i].reshape(1, -1),
            w_gate=ffn_w_gate[i].astype(BF16), w_up=ffn_w_up[i].astype(BF16), w_down=ffn_w_down[i].astype(BF16),
            fconv_w=ffn_conv_w[i], fconv_b=ffn_conv_b[i].reshape(1, -1),
        )
        lam_init = 0.8 - 0.6 * math.exp(-0.3 * i)
        xp, st_p = _layer(xp, None, i, p, lam_init, bias_prompt, bias_sample)
        xs, st_s = _layer(xs, cache, i, p, lam_init, bias_prompt, bias_sample)
        p_states.append(st_p)
        s_states.append(st_s)

    g_fin = final_norm.reshape(1, d)
    y_prompt = _final_norm(xp, g_fin, n_meta)
    y_sample = _final_norm(xs, g_fin, 0)
    stack = lambda states: tuple(jnp.stack([s[j] for s in states], axis=0) for j in range(7))
    return (y_prompt, y_sample) + stack(p_states) + stack(s_states)
```

```python
import functools
import math

import jax
import jax.numpy as jnp
from jax import lax
from jax.experimental import pallas as pl
from jax.experimental.pallas import tpu as pltpu

F32 = jnp.float32
BF16 = jnp.bfloat16

LANES = 128
SUBLANES = 8
BF16_ROWS = 16
V7X_VMEM_BYTES = 64 * 1024 * 1024
VMEM_LIMIT = 56 * 1024 * 1024

CHUNK = 64
N_META = 16
EPS = 1e-6
NEG_INF = -1e30
DA_HEADS = 4
DA_HEAD_DIM = 64
DA_VDIM = 128
SB_HEADS = 8
SB_HEAD_DIM = 64
SSM_HEADS = 8
SSM_HEAD_DIM = 64
SSM_GROUPS = 2
SSM_D_STATE = 128
SSM_CONV = 4
FFN_CONV = 3
REL_BUCKETS = 32
REL_MAX_DIST = 128
QK_SCALE = 0.125
SSD_BLOCK = 128
FF_CHUNK = 256
KEY_GROUP = 4
GROUP_KEYS = KEY_GROUP * LANES
QUERY_GROUP = KEY_GROUP
SUPER_ROWS = QUERY_GROUP * LANES
BIAS_HIDDEN = 4


def _dot(a, b):
    return jnp.dot(a, b, preferred_element_type=F32)


def _dot_nt(a, b):
    return lax.dot_general(a, b, (((1,), (1,)), ((), ())), preferred_element_type=F32)


def _split3(x):
    hi = x.astype(BF16)
    r1 = x - hi.astype(F32)
    mid = r1.astype(BF16)
    lo = (r1 - mid.astype(F32)).astype(BF16)
    return hi, mid, lo


def _dot_exact_rhs(x, m_bf16):
    hi, mid, lo = _split3(x)
    return _dot(hi, m_bf16) + _dot(mid, m_bf16) + _dot(lo, m_bf16)


def _dot_exact_lhs(m_bf16, x):
    hi, mid, lo = _split3(x)
    return _dot(m_bf16, hi) + _dot(m_bf16, mid) + _dot(m_bf16, lo)


def _rms(x, g):
    return x * lax.rsqrt(jnp.mean(x * x, axis=-1, keepdims=True) + EPS) * g


def _silu(x):
    return x * (1.0 / (1.0 + jnp.exp(-x)))


def _softplus(x):
    return jnp.maximum(x, 0.0) + jnp.log1p(jnp.exp(-jnp.abs(x)))


def _pick_tile(n, cap, mult):
    best = None
    for t in range(mult, min(n, cap) + 1, mult):
        if n % t == 0:
            best = t
    assert best is not None, (n, cap, mult)
    return best


def _resident(shape):
    nd = len(shape)
    return pl.BlockSpec(shape, lambda *_: (0,) * nd, pipeline_mode=pl.Buffered(1))


def _params(sem):
    return pltpu.CompilerParams(dimension_semantics=sem, vmem_limit_bytes=VMEM_LIMIT)


def _rel_bucket(rel):
    half = REL_BUCKETS // 2
    max_exact = half // 2
    n = jnp.abs(rel)
    nf = jnp.maximum(n, 1).astype(jnp.float32)
    large = max_exact + (jnp.log(nf / max_exact) / math.log(REL_MAX_DIST / max_exact)
                         * (half - max_exact)).astype(jnp.int32)
    large = jnp.minimum(large, half - 1)
    return jnp.where(rel > 0, half, 0) + jnp.where(n < max_exact, n, large)


def _bias_kernel(tab_ref, bp_ref, bs_ref, op_ref, os_ref, *, past, n_new):
    h = pl.program_id(0)

    def lookup(bucket):
        val = jnp.zeros(bucket.shape, F32)
        for b in range(REL_BUCKETS):
            val = jnp.where(bucket == b, tab_ref[b, h], val)
        return val

    r = lax.broadcasted_iota(jnp.int32, (LANES, LANES), 0)
    c = lax.broadcasted_iota(jnp.int32, (LANES, LANES), 1)
    qc = jnp.right_shift(r - N_META, 6)
    kc = jnp.right_shift(c - N_META, 6)
    for slot in range(BIAS_HIDDEN):
        val = lookup(bp_ref[slot])
        if slot > 0:
            d = slot - 2
            val = jnp.where(2 * d + kc - qc <= 0, val, NEG_INF)
        op_ref[0, slot] = val
    op_ref[0, BIAS_HIDDEN] = jnp.full((LANES, LANES), NEG_INF, F32)

    lw = bs_ref.shape[1]
    rs = lax.broadcasted_iota(jnp.int32, (n_new, lw), 0) + past
    cs = lax.broadcasted_iota(jnp.int32, (n_new, lw), 1)
    vis = (jnp.right_shift(cs, 6) <= jnp.right_shift(rs, 6)) & (cs < past + n_new)
    os_ref[0] = jnp.where(vis, lookup(bs_ref[...]), NEG_INF)


def _bias_tiles(rel_table, past, n_new):
    lw = pl.cdiv(past + n_new, LANES) * LANES
    r = jnp.arange(LANES, dtype=jnp.int32)[:, None]
    c = jnp.arange(LANES, dtype=jnp.int32)[None, :]
    far = jnp.full((LANES, LANES), -2 * LANES, jnp.int32)
    bp = jnp.stack([_rel_bucket(far)] + [_rel_bucket(d * LANES + c - r) for d in (-1, 0, 1)])
    qs = past + jnp.arange(n_new, dtype=jnp.int32)[:, None]
    ks = jnp.arange(lw, dtype=jnp.int32)[None, :]
    bs = _rel_bucket(ks - qs)
    return pl.pallas_call(
        functools.partial(_bias_kernel, past=past, n_new=n_new),
        grid=(DA_HEADS,),
        in_specs=[pl.BlockSpec(memory_space=pltpu.SMEM),
                  pl.BlockSpec((4, LANES, LANES), lambda h: (0, 0, 0)),
                  pl.BlockSpec((n_new, lw), lambda h: (0, 0))],
        out_specs=[pl.BlockSpec((1, BIAS_HIDDEN + 1, LANES, LANES), lambda h: (h, 0, 0, 0)),
                   pl.BlockSpec((1, n_new, lw), lambda h: (h, 0, 0))],
        out_shape=[jax.ShapeDtypeStruct((DA_HEADS, BIAS_HIDDEN + 1, LANES, LANES), F32),
                   jax.ShapeDtypeStruct((DA_HEADS, n_new, lw), F32)],
        name="rel_bias_tiles",
    )(rel_table, bp, bs)


_SEG = (("da_q", 512, BF16), ("da_k", 512, F32), ("da_v", 512, F32), ("ssm_z", 512, F32),
        ("ssm_xbc", 1024, F32), ("sb_q", 512, BF16), ("sb_k", 512, F32), ("sb_v", 512, F32),
        ("ssm_dt", LANES, F32))
_N_PROJ = sum(s[1] for s in _SEG)
_OFF_DT = 3072


def _inproj_kernel(x_ref, g_ref, w_ref, *out_refs):
    h = _rms(x_ref[...], g_ref[...]).astype(BF16)
    off = 0
    for (_, n, dt), o_ref in zip(_SEG, out_refs):
        o_ref[...] = _dot(h, w_ref[:, off:off + n]).astype(dt)
        off += n


def _in_proj(x2d, g, w_r):
    t, d = x2d.shape
    tm = _pick_tile(t, 256, BF16_ROWS)
    return pl.pallas_call(
        _inproj_kernel,
        grid=(t // tm,),
        in_specs=[pl.BlockSpec((tm, d), lambda i: (i, 0)), _resident((1, d)), _resident((d, _N_PROJ))],
        out_specs=[pl.BlockSpec((tm, n), lambda i: (i, 0)) for _, n, _ in _SEG],
        out_shape=[jax.ShapeDtypeStruct((t, n), dt) for _, n, dt in _SEG],
        compiler_params=_params(("arbitrary",)),
        name="in_proj",
    )(x2d, g, w_r)


def _lambda(lamp_ref, lam_init):
    lp = lamp_ref[...]
    return (jnp.exp(jnp.sum(lp[0:1] * lp[1:2], axis=1, keepdims=True))
            - jnp.exp(jnp.sum(lp[2:3] * lp[3:4], axis=1, keepdims=True)) + lam_init)


def _split_maps(q_bf16):
    lane = lax.broadcasted_iota(jnp.int32, (1, LANES), 1)
    qf = q_bf16.astype(F32) * QK_SCALE
    return (jnp.where(lane < DA_HEAD_DIM, qf, 0.0).astype(BF16),
            jnp.where(lane >= DA_HEAD_DIM, qf, 0.0).astype(BF16))


def _da_prompt_kernel(q_ref, k_ref, v_ref, bm_ref, lamp_ref, subln_ref, o_ref,
                      q1s, q2s, ks, vs, sbuf, sbig, mx_ref, l_ref, acc_ref, *, seq, lam_init):
    n_full, tail = divmod(seq, LANES)
    n_blk = n_full + (1 if tail else 0)
    lp = ks.shape[0]
    q1, q2 = _split_maps(q_ref[0])
    q1s[...] = q1
    q2s[...] = q2
    ks[0:seq, :] = k_ref[0].astype(BF16)
    vs[0:seq, :] = v_ref[0].astype(BF16)
    ks[seq:lp, :] = jnp.zeros((lp - seq, LANES), BF16)
    vs[seq:lp, :] = jnp.zeros((lp - seq, LANES), BF16)
    lam = _lambda(lamp_ref, lam_init)
    gsub = subln_ref[...] * (1.0 - lam_init)

    def tile(i, rows, r0):
        q1t = q1s[pl.ds(r0, rows), :]
        q2t = q2s[pl.ds(r0, rows), :]
        n_grp = (jnp.minimum(i + 2, n_blk) + KEY_GROUP - 1) // KEY_GROUP

        def scores(grp, carry):
            m1, m2 = carry
            base = pl.multiple_of(grp * GROUP_KEYS, GROUP_KEYS)
            for g in range(KEY_GROUP):
                kb = grp * KEY_GROUP + g
                kblk = ks[pl.ds(base + g * LANES, LANES), :]
                slot = jnp.where(kb >= n_blk, BIAS_HIDDEN, jnp.clip(kb - i + 2, 0, BIAS_HIDDEN))
                bias = bm_ref[0, slot, 0:rows, :]
                s1 = _dot_nt(q1t, kblk) + bias
                s2 = _dot_nt(q2t, kblk) + bias
                sbuf[0, grp, 0:rows, g * LANES:(g + 1) * LANES] = s1
                sbuf[1, grp, 0:rows, g * LANES:(g + 1) * LANES] = s2
                m1 = jnp.maximum(m1, s1)
                m2 = jnp.maximum(m2, s2)
            return m1, m2

        ninf = jnp.full((rows, LANES), -jnp.inf, F32)
        m1, m2 = lax.fori_loop(0, n_grp, scores, (ninf, ninf))
        m1 = jnp.max(m1, axis=1, keepdims=True)
        m2 = jnp.max(m2, axis=1, keepdims=True)

        def exp_pv(grp, carry):
            l1, l2, a1, a2 = carry
            base = pl.multiple_of(grp * GROUP_KEYS, GROUP_KEYS)
            ps1, ps2 = [], []
            for g in range(KEY_GROUP):
                p1 = jnp.exp(sbuf[0, grp, 0:rows, g * LANES:(g + 1) * LANES] - m1)
                p2 = jnp.exp(sbuf[1, grp, 0:rows, g * LANES:(g + 1) * LANES] - m2)
                l1 = l1 + p1
                l2 = l2 + p2
                ps1.append(p1.astype(BF16))
                ps2.append(p2.astype(BF16))
            vrows = vs[pl.ds(base, GROUP_KEYS), :]
            a1 = a1 + _dot(jnp.concatenate(ps1, axis=1), vrows)
            a2 = a2 + _dot(jnp.concatenate(ps2, axis=1), vrows)
            return l1, l2, a1, a2

        zero = jnp.zeros((rows, LANES), F32)
        l1, l2, a1, a2 = lax.fori_loop(0, n_grp, exp_pv, (zero, zero, zero, zero))
        c1 = 1.0 / jnp.sum(l1, axis=1, keepdims=True)
        c2 = lam / jnp.sum(l2, axis=1, keepdims=True)
        o_ref[0, pl.ds(r0, rows), :] = _rms(a1 * c1 - a2 * c2, gsub).astype(o_ref.dtype)

    def super_tile(a, carry):
        r_base = pl.multiple_of(a * SUPER_ROWS, SUPER_ROWS)
        k_base = pl.multiple_of(a * GROUP_KEYS, GROUP_KEYS)
        maps = (q1s, q2s)
        ninf = jnp.full((LANES, LANES), -jnp.inf, F32)
        zero = jnp.zeros((LANES, LANES), F32)
        for m in range(2):
            for tq in range(QUERY_GROUP):
                mx_ref[m, tq] = ninf
                l_ref[m, tq] = zero
                acc_ref[m, tq] = zero

        def scores_lower(grp, c):
            base = pl.multiple_of(grp * GROUP_KEYS, GROUP_KEYS)
            for m in range(2):
                qsup = maps[m][pl.ds(r_base, SUPER_ROWS), :]
                mx = [mx_ref[m, tq] for tq in range(QUERY_GROUP)]
                for g in range(KEY_GROUP):
                    kb = grp * KEY_GROUP + g
                    s_all = _dot_nt(qsup, ks[pl.ds(base + g * LANES, LANES), :])
                    for tq in range(QUERY_GROUP):
                        slot = jnp.clip(kb - (a * QUERY_GROUP + tq) + 2, 0, BIAS_HIDDEN)
                        s = s_all[tq * LANES:(tq + 1) * LANES] + bm_ref[0, slot]
                        sbig[m, tq, kb] = s
                        mx[tq] = jnp.maximum(mx[tq], s)
                for tq in range(QUERY_GROUP):
                    mx_ref[m, tq] = mx[tq]
            return c

        lax.fori_loop(0, a, scores_lower, 0)

        for m in range(2):
            qsup = maps[m][pl.ds(r_base, SUPER_ROWS), :]
            mx = [mx_ref[m, tq] for tq in range(QUERY_GROUP)]
            for g in range(KEY_GROUP + 1):
                first = max(g - 1, 0)
                kb = a * KEY_GROUP + g
                s_all = _dot_nt(qsup[first * LANES:, :], ks[pl.ds(k_base + g * LANES, LANES), :])
                for tq in range(first, QUERY_GROUP):
                    slot = 0 if g - tq <= -2 else g - tq + 2
                    if g == KEY_GROUP:
                        slot = jnp.where(kb >= n_blk, BIAS_HIDDEN, slot)
                    s = s_all[(tq - first) * LANES:(tq - first + 1) * LANES] + bm_ref[0, slot]
                    sbig[m, tq, kb] = s
                    mx[tq] = jnp.maximum(mx[tq], s)
            for tq in range(QUERY_GROUP):
                mx_ref[m, tq] = jnp.broadcast_to(jnp.max(mx[tq], axis=1, keepdims=True), (LANES, LANES))

        def pv_lower(grp, c):
            base = pl.multiple_of(grp * GROUP_KEYS, GROUP_KEYS)
            vrows = vs[pl.ds(base, GROUP_KEYS), :]
            for m in range(2):
                prows = []
                for tq in range(QUERY_GROUP):
                    mrow = mx_ref[m, tq]
                    lsum = l_ref[m, tq]
                    ps = []
                    for g in range(KEY_GROUP):
                        p = jnp.exp(sbig[m, tq, grp * KEY_GROUP + g] - mrow)
                        lsum = lsum + p
                        ps.append(p.astype(BF16))
                    l_ref[m, tq] = lsum
                    prows.append(jnp.concatenate(ps, axis=1))
                pv = _dot(jnp.concatenate(prows, axis=0), vrows)
                for tq in range(QUERY_GROUP):
                    acc_ref[m, tq] += pv[tq * LANES:(tq + 1) * LANES]
            return c

        lax.fori_loop(0, a, pv_lower, 0)

        for tq in range(QUERY_GROUP):
            nb = tq + 2
            vrows = vs[pl.ds(k_base, nb * LANES), :]
            outs = []
            for m in range(2):
                mrow = mx_ref[m, tq]
                lsum = l_ref[m, tq]
                ps = []
                for g in range(nb):
                    p = jnp.exp(sbig[m, tq, a * KEY_GROUP + g] - mrow)
                    lsum = lsum + p
                    ps.append(p.astype(BF16))
                acc = acc_ref[m, tq] + _dot(jnp.concatenate(ps, axis=1), vrows)
                outs.append((acc, jnp.sum(lsum, axis=1, keepdims=True)))
            o = outs[0][0] * (1.0 / outs[0][1]) - outs[1][0] * (lam / outs[1][1])
            o_ref[0, pl.ds(r_base + tq * LANES, LANES), :] = _rms(o, gsub).astype(o_ref.dtype)
        return carry

    def full_tile(i, carry):
        tile(i, LANES, pl.multiple_of(i * LANES, LANES))
        return carry

    n_super = n_full // QUERY_GROUP
    lax.fori_loop(0, n_super, super_tile, 0)
    lax.fori_loop(n_super * QUERY_GROUP, n_full, full_tile, 0)
    if tail:
        r0 = n_full * LANES
        nk = n_blk * LANES
        tiles = [bm_ref[0, 0, 0:tail, :]] * max(n_full - 1, 0)
        if n_full >= 1:
            tiles.append(bm_ref[0, 1, 0:tail, :])
        tiles.append(bm_ref[0, 2, 0:tail, :])
        bias = tiles[0] if len(tiles) == 1 else jnp.concatenate(tiles, axis=1)
        kall = ks[0:nk, :]
        s1 = _dot_nt(q1s[r0:r0 + tail, :], kall) + bias
        s2 = _dot_nt(q2s[r0:r0 + tail, :], kall) + bias
        p1 = jnp.exp(s1 - jnp.max(s1, axis=1, keepdims=True))
        p2 = jnp.exp(s2 - jnp.max(s2, axis=1, keepdims=True))
        c1 = 1.0 / jnp.sum(p1, axis=1, keepdims=True)
        c2 = lam / jnp.sum(p2, axis=1, keepdims=True)
        acc = _dot((p1 * c1 - p2 * c2).astype(BF16), vs[0:nk, :])
        o_ref[0, r0:r0 + tail, :] = _rms(acc, gsub).astype(o_ref.dtype)


def _da_prompt(q, k, v, bm, lamp, subln, lam_init):
    b, seq, _ = q.shape
    assert seq % BF16_ROWS == 0 and (seq - N_META) % CHUNK == 0
    n_blk = pl.cdiv(seq, LANES)
    n_grp = pl.cdiv(n_blk + 1, KEY_GROUP)
    lp = n_grp * GROUP_KEYS
    n_big = (seq // SUPER_ROWS) * QUERY_GROUP + 1
    stat = lambda: pltpu.VMEM((2, QUERY_GROUP, LANES, LANES), F32)
    blk = lambda: pl.BlockSpec((1, seq, LANES), lambda i, h: (i, 0, h))
    return pl.pallas_call(
        functools.partial(_da_prompt_kernel, seq=seq, lam_init=lam_init),
        grid=(b, DA_HEADS),
        in_specs=[blk(), blk(), blk(),
                  pl.BlockSpec((1, BIAS_HIDDEN + 1, LANES, LANES), lambda i, h: (h, 0, 0, 0)),
                  pl.BlockSpec((4, DA_HEAD_DIM), lambda i, h: (0, 0)),
                  pl.BlockSpec((1, DA_VDIM), lambda i, h: (0, 0))],
        out_specs=blk(),
        out_shape=jax.ShapeDtypeStruct((b, seq, DA_HEADS * DA_VDIM), BF16),
        scratch_shapes=[pltpu.VMEM((seq, LANES), BF16), pltpu.VMEM((seq, LANES), BF16),
                        pltpu.VMEM((lp, LANES), BF16), pltpu.VMEM((lp, LANES), BF16),
                        pltpu.VMEM((2, n_grp, LANES, GROUP_KEYS), F32),
                        pltpu.VMEM((2, QUERY_GROUP, n_big, LANES, LANES), F32), stat(), stat(), stat()],
        compiler_params=_params(("arbitrary", "arbitrary")),
        name="diff_attn_prompt",
    )(q, k, v, bm, lamp, subln)


def _da_sample_kernel(q_ref, k_ref, v_ref, ck_ref, cv_ref, bs_ref, lamp_ref, subln_ref, o_ref,
                      ks, vs, *, past, n_new, lam_init):
    lw = ks.shape[0]
    tot = past + n_new
    ks[0:past, :] = ck_ref[0, 0].astype(BF16)
    vs[0:past, :] = cv_ref[0, 0].astype(BF16)
    ks[past:tot, :] = k_ref[0].astype(BF16)
    vs[past:tot, :] = v_ref[0].astype(BF16)
    if lw > tot:
        ks[tot:lw, :] = jnp.zeros((lw - tot, LANES), BF16)
        vs[tot:lw, :] = jnp.zeros((lw - tot, LANES), BF16)
    q1, q2 = _split_maps(q_ref[0])
    bias = bs_ref[0]
    kall = ks[...]
    s1 = _dot_nt(q1, kall) + bias
    s2 = _dot_nt(q2, kall) + bias
    p1 = jnp.exp(s1 - jnp.max(s1, axis=1, keepdims=True))
    p2 = jnp.exp(s2 - jnp.max(s2, axis=1, keepdims=True))
    c1 = 1.0 / jnp.sum(p1, axis=1, keepdims=True)
    c2 = _lambda(lamp_ref, lam_init) / jnp.sum(p2, axis=1, keepdims=True)
    acc = _dot((p1 * c1 - p2 * c2).astype(BF16), vs[...])
    o_ref[0] = _rms(acc, subln_ref[...] * (1.0 - lam_init)).astype(o_ref.dtype)


def _da_sample(q, k, v, cache_k, cache_v, layer, bs, lamp, subln, lam_init):
    b, n_new, _ = q.shape
    past = cache_k.shape[2]
    assert past % BF16_ROWS == 0 and n_new % BF16_ROWS == 0
    lw = bs.shape[2]
    new = lambda: pl.BlockSpec((1, n_new, LANES), lambda i, h: (i, 0, h))
    old = lambda: pl.BlockSpec((1, 1, past, LANES), lambda i, h: (layer, i, 0, h))
    return pl.pallas_call(
        functools.partial(_da_sample_kernel, past=past, n_new=n_new, lam_init=lam_init),
        grid=(b, DA_HEADS),
        in_specs=[new(), new(), new(), old(), old(),
                  pl.BlockSpec((1, n_new, lw), lambda i, h: (h, 0, 0)),
                  pl.BlockSpec((4, DA_HEAD_DIM), lambda i, h: (0, 0)),
                  pl.BlockSpec((1, DA_VDIM), lambda i, h: (0, 0))],
        out_specs=new(),
        out_shape=jax.ShapeDtypeStruct((b, n_new, DA_HEADS * DA_VDIM), BF16),
        scratch_shapes=[pltpu.VMEM((lw, LANES), BF16), pltpu.VMEM((lw, LANES), BF16)],
        compiler_params=_params(("arbitrary", "arbitrary")),
        name="diff_attn_sample",
    )(q, k, v, cache_k, cache_v, bs, lamp, subln)


def _split_heads(q_bf16):
    lane = lax.broadcasted_iota(jnp.int32, (1, LANES), 1)
    qf = q_bf16.astype(F32) * QK_SCALE
    return (jnp.where(lane < SB_HEAD_DIM, qf, 0.0).astype(BF16),
            jnp.where(lane >= SB_HEAD_DIM, qf, 0.0).astype(BF16))


def _sb_tile(i, rows, qts, ks, vs, suffix):
    row = lax.broadcasted_iota(jnp.int32, (rows, LANES), 0)
    col = lax.broadcasted_iota(jnp.int32, (rows, LANES), 1)
    col_minus_row = col - row
    top = i // KEY_GROUP

    def group(grp, runs, accs, diagonal):
        base = pl.multiple_of(grp * GROUP_KEYS, GROUP_KEYS)
        vrows = vs[pl.ds(base, GROUP_KEYS), :]
        new_runs, new_accs = [], []
        for qt, run, acc in zip(qts, runs, accs):
            log_betas, inners, totals, visible = [], [], [], []
            for g in range(KEY_GROUP):
                z = _dot_nt(qt, ks[pl.ds(base + g * LANES, LANES), :])
                log_beta = jnp.minimum(z, 0.0) - jnp.log(1.0 + jnp.exp(-jnp.abs(z)))
                log_keep = log_beta - z
                if diagonal:
                    vis = col_minus_row < (i - grp * KEY_GROUP - g) * LANES
                    log_keep = jnp.where(vis, log_keep, 0.0)
                    visible.append(vis)
                hi = log_keep.astype(BF16)
                lo = (log_keep - hi.astype(F32)).astype(BF16)
                both = _dot(jnp.concatenate([hi, lo], axis=1), suffix)
                log_betas.append(log_beta)
                inners.append(both[:, 0:LANES])
                totals.append(both[:, LANES:])
            ws = [None] * KEY_GROUP
            for g in reversed(range(KEY_GROUP)):
                w = jnp.exp(log_betas[g] + inners[g] + run)
                if diagonal:
                    w = jnp.where(visible[g], w, 0.0)
                ws[g] = w.astype(BF16)
                run = run + totals[g]
            new_runs.append(run)
            new_accs.append(acc + _dot(jnp.concatenate(ws, axis=1), vrows))
        return new_runs, new_accs

    zero = jnp.zeros((rows, LANES), F32)
    runs, accs = group(top, [zero, zero], [zero, zero], True)

    def body(t, carry):
        r, a = group(top - 1 - t, carry[0:2], carry[2:4], False)
        return (r[0], r[1], a[0], a[1])

    carry = lax.fori_loop(0, top, body, (runs[0], runs[1], accs[0], accs[1]))
    lane = lax.broadcasted_iota(jnp.int32, (1, LANES), 1)
    return jnp.where(lane < SB_HEAD_DIM, carry[2], carry[3])


def _sb_rows_static(i, rows, qts, ks, vs, suffix):
    nk = (i + 1) * LANES
    row = lax.broadcasted_iota(jnp.int32, (rows, LANES), 0)
    col = lax.broadcasted_iota(jnp.int32, (rows, LANES), 1)
    vis = col < row
    kall = ks[0:nk, :]
    vall = vs[0:nk, :]
    outs = []
    for qt in qts:
        z = _dot_nt(qt, kall)
        log_beta = jnp.minimum(z, 0.0) - jnp.log(1.0 + jnp.exp(-jnp.abs(z)))
        log_keep = log_beta - z
        sums = []
        for g in range(i + 1):
            lk = log_keep[:, g * LANES:(g + 1) * LANES]
            if g == i:
                lk = jnp.where(vis, lk, 0.0)
            hi = lk.astype(BF16)
            lo = (lk - hi.astype(F32)).astype(BF16)
            sums.append(_dot(jnp.concatenate([hi, lo], axis=1), suffix))
        run = jnp.zeros((rows, LANES), F32)
        ws = [None] * (i + 1)
        for g in reversed(range(i + 1)):
            w = jnp.exp(log_beta[:, g * LANES:(g + 1) * LANES] + sums[g][:, 0:LANES] + run)
            if g == i:
                w = jnp.where(vis, w, 0.0)
            ws[g] = w.astype(BF16)
            run = run + sums[g][:, LANES:]
        outs.append(_dot(jnp.concatenate(ws, axis=1), vall))
    lane = lax.broadcasted_iota(jnp.int32, (1, LANES), 1)
    return jnp.where(lane < SB_HEAD_DIM, outs[0], outs[1])


def _suffix_matrix():
    r = lax.broadcasted_iota(jnp.int32, (2 * LANES, 2 * LANES), 0)
    c = lax.broadcasted_iota(jnp.int32, (2 * LANES, 2 * LANES), 1)
    later = jnp.bitwise_and(r, LANES - 1) > c
    return jnp.where((c >= LANES) | later, 1.0, 0.0).astype(BF16)


def _sb_prompt_kernel(q_ref, k_ref, v_ref, o_ref, qas, qbs, ks, vs, run_ref, acc_ref, *, seq):
    n_full, tail = divmod(seq, LANES)
    lp = ks.shape[0]
    qa, qb = _split_heads(q_ref[0])
    qas[...] = qa
    qbs[...] = qb
    ks[0:seq, :] = k_ref[0].astype(BF16)
    vs[0:seq, :] = v_ref[0].astype(BF16)
    if lp > seq:
        ks[seq:lp, :] = jnp.zeros((lp - seq, LANES), BF16)
        vs[seq:lp, :] = jnp.zeros((lp - seq, LANES), BF16)
    suffix = _suffix_matrix()

    row = lax.broadcasted_iota(jnp.int32, (LANES, LANES), 0)
    col = lax.broadcasted_iota(jnp.int32, (LANES, LANES), 1)
    vis = col < row
    lane = lax.broadcasted_iota(jnp.int32, (1, LANES), 1)

    def super_tile(a, carry):
        r_base = pl.multiple_of(a * SUPER_ROWS, SUPER_ROWS)
        zero = jnp.zeros((LANES, LANES), F32)
        for tq in range(QUERY_GROUP):
            for h in range(2):
                run_ref[tq, h] = zero
                acc_ref[tq, h] = zero

        def group(grp, diagonal):
            base = pl.multiple_of(grp * GROUP_KEYS, GROUP_KEYS)
            vrows = vs[pl.ds(base, GROUP_KEYS), :]
            for h, qs in enumerate((qas, qbs)):
                qsup = qs[pl.ds(r_base, SUPER_ROWS), :]
                log_betas = [[None] * KEY_GROUP for _ in range(QUERY_GROUP)]
                inners = [[None] * KEY_GROUP for _ in range(QUERY_GROUP)]
                totals = [[None] * KEY_GROUP for _ in range(QUERY_GROUP)]
                for g in range(KEY_GROUP):
                    first = g if diagonal else 0
                    z = _dot_nt(qsup[first * LANES:, :], ks[pl.ds(base + g * LANES, LANES), :])
                    log_beta = jnp.minimum(z, 0.0) - jnp.log(1.0 + jnp.exp(-jnp.abs(z)))
                    log_keep = log_beta - z
                    if diagonal:
                        top = jnp.where(vis, log_keep[0:LANES], 0.0)
                        log_keep = top if first == QUERY_GROUP - 1 else jnp.concatenate([top, log_keep[LANES:]], axis=0)
                    hi = log_keep.astype(BF16)
                    lo = (log_keep - hi.astype(F32)).astype(BF16)
                    both = _dot(jnp.concatenate([hi, lo], axis=1), suffix)
                    for tq in range(first, QUERY_GROUP):
                        s0 = (tq - first) * LANES
                        log_betas[tq][g] = log_beta[s0:s0 + LANES]
                        inners[tq][g] = both[s0:s0 + LANES, 0:LANES]
                        totals[tq][g] = both[s0:s0 + LANES, LANES:]
                weights = []
                for tq in range(QUERY_GROUP):
                    run = run_ref[tq, h]
                    last = tq if diagonal else KEY_GROUP - 1
                    ws = [None] * (last + 1)
                    for g in reversed(range(last + 1)):
                        w = jnp.exp(log_betas[tq][g] + inners[tq][g] + run)
                        if diagonal and g == tq:
                            w = jnp.where(vis, w, 0.0)
                        ws[g] = w.astype(BF16)
                        run = run + totals[tq][g]
                    run_ref[tq, h] = run
                    wrow = ws[0] if last == 0 else jnp.concatenate(ws, axis=1)
                    if diagonal:
                        acc_ref[tq, h] += _dot(wrow, vrows[0:(last + 1) * LANES, :])
                    else:
                        weights.append(wrow)
                if not diagonal:
                    pv = _dot(jnp.concatenate(weights, axis=0), vrows)
                    for tq in range(QUERY_GROUP):
                        acc_ref[tq, h] += pv[tq * LANES:(tq + 1) * LANES]

        group(a, True)

        def lower(t, c):
            group(a - 1 - t, False)
            return c

        lax.fori_loop(0, a, lower, 0)
        for tq in range(QUERY_GROUP):
            o = jnp.where(lane < SB_HEAD_DIM, acc_ref[tq, 0], acc_ref[tq, 1])
            o_ref[0, pl.ds(r_base + tq * LANES, LANES), :] = o.astype(o_ref.dtype)
        return carry

    def full_tile(i, carry):
        r0 = pl.multiple_of(i * LANES, LANES)
        o = _sb_tile(i, LANES, (qas[pl.ds(r0, LANES), :], qbs[pl.ds(r0, LANES), :]), ks, vs, suffix)
        o_ref[0, pl.ds(r0, LANES), :] = o.astype(o_ref.dtype)
        return carry

    n_super = n_full // QUERY_GROUP
    lax.fori_loop(0, n_super, super_tile, 0)
    lax.fori_loop(n_super * QUERY_GROUP, n_full, full_tile, 0)
    if tail:
        r0 = n_full * LANES
        o = _sb_rows_static(n_full, tail, (qas[r0:r0 + tail, :], qbs[r0:r0 + tail, :]), ks, vs, suffix)
        o_ref[0, r0:r0 + tail, :] = o.astype(o_ref.dtype)


def _sb_prompt(q, k, v):
    b, seq, _ = q.shape
    assert seq % BF16_ROWS == 0
    lp = pl.cdiv(pl.cdiv(seq, LANES), KEY_GROUP) * GROUP_KEYS
    blk = lambda: pl.BlockSpec((1, seq, LANES), lambda i, p: (i, 0, p))
    return pl.pallas_call(
        functools.partial(_sb_prompt_kernel, seq=seq),
        grid=(b, SB_HEADS // 2),
        in_specs=[blk(), blk(), blk()],
        out_specs=blk(),
        out_shape=jax.ShapeDtypeStruct((b, seq, SB_HEADS * SB_HEAD_DIM), BF16),
        scratch_shapes=[pltpu.VMEM((seq, LANES), BF16), pltpu.VMEM((seq, LANES), BF16),
                        pltpu.VMEM((lp, LANES), BF16), pltpu.VMEM((lp, LANES), BF16),
                        pltpu.VMEM((QUERY_GROUP, 2, LANES, LANES), F32),
                        pltpu.VMEM((QUERY_GROUP, 2, LANES, LANES), F32)],
        compiler_params=_params(("arbitrary", "arbitrary")),
        name="stick_prompt",
    )(q, k, v)


def _sb_sample_kernel(q_ref, k_ref, v_ref, ck_ref, cv_ref, o_ref, ks, vs, *, past, n_new):
    lw = ks.shape[0]
    tot = past + n_new
    ks[0:past, :] = ck_ref[0, 0].astype(BF16)
    vs[0:past, :] = cv_ref[0, 0].astype(BF16)
    ks[past:tot, :] = k_ref[0].astype(BF16)
    vs[past:tot, :] = v_ref[0].astype(BF16)
    if lw > tot:
        ks[tot:lw, :] = jnp.zeros((lw - tot, LANES), BF16)
        vs[tot:lw, :] = jnp.zeros((lw - tot, LANES), BF16)
    o = _sb_rows_static(past // LANES, n_new, _split_heads(q_ref[0]), ks, vs, _suffix_matrix())
    o_ref[0] = o.astype(o_ref.dtype)


def _sb_sample(q, k, v, cache_k, cache_v, layer):
    b, n_new, _ = q.shape
    past = cache_k.shape[2]
    assert past % LANES == 0 and n_new <= LANES and n_new % BF16_ROWS == 0
    lw = (past // GROUP_KEYS + 1) * GROUP_KEYS
    new = lambda: pl.BlockSpec((1, n_new, LANES), lambda i, p: (i, 0, p))
    old = lambda: pl.BlockSpec((1, 1, past, LANES), lambda i, p: (layer, i, 0, p))
    return pl.pallas_call(
        functools.partial(_sb_sample_kernel, past=past, n_new=n_new),
        grid=(b, SB_HEADS // 2),
        in_specs=[new(), new(), new(), old(), old()],
        out_specs=new(),
        out_shape=jax.ShapeDtypeStruct((b, n_new, SB_HEADS * SB_HEAD_DIM), BF16),
        scratch_shapes=[pltpu.VMEM((lw, LANES), BF16), pltpu.VMEM((lw, LANES), BF16)],
        compiler_params=_params(("arbitrary", "arbitrary")),
        name="stick_sample",
    )(q, k, v, cache_k, cache_v)


def _ssd_kernel(xbc_ref, z_ref, dt_ref, hist_ref, h0_ref, cw_ref, cb_ref, dtb_ref, alog_ref,
                dexp_ref, gn_ref, y_ref, hfin_ref, prev8, ht, *, seq):
    d_in = SSM_HEADS * SSM_HEAD_DIM
    gw = d_in // SSM_GROUPS
    n_full, tail = divmod(seq, SSD_BLOCK)
    blk = SSD_BLOCK

    r = lax.broadcasted_iota(jnp.int32, (blk, blk), 0)
    c = lax.broadcasted_iota(jnp.int32, (blk, blk), 1)
    tril = r >= c
    tri = jnp.where(tril, 1.0, 0.0).astype(BF16)
    er = lax.broadcasted_iota(jnp.int32, (LANES, d_in), 0)
    ec = lax.broadcasted_iota(jnp.int32, (LANES, d_in), 1)
    expand = jnp.where(jnp.right_shift(ec, 6) == er, 1.0, 0.0).astype(BF16)
    lane = lax.broadcasted_iota(jnp.int32, (1, LANES), 1)

    prev8[...] = hist_ref[0]
    for g in range(SSM_GROUPS):
        for j in range(gw // LANES):
            rows = h0_ref[0, g * gw + j * LANES:g * gw + (j + 1) * LANES, :]
            ht[g, :, j * LANES:(j + 1) * LANES] = rows.T

    a_neg = -jnp.exp(alog_ref[...])
    cw = cw_ref[...]

    def block(t0, rows):
        xin = xbc_ref[0, pl.ds(t0, rows), :]
        win = jnp.concatenate([prev8[...], xin], axis=0)
        conv = cw[3:4] * win
        for k in range(1, SSM_CONV):
            conv = conv + cw[3 - k:4 - k] * pltpu.roll(win, k, axis=0)
        xbc = _silu(conv[SUBLANES:] + cb_ref[...])
        prev8[...] = xin[rows - SUBLANES:rows]
        dt = _softplus(dt_ref[0, pl.ds(t0, rows), :] + dtb_ref[...])
        z = z_ref[0, pl.ds(t0, rows), :]
        if rows < blk:
            pad = blk - rows
            xbc = jnp.concatenate([xbc, jnp.zeros((pad, xbc.shape[1]), F32)], axis=0)
            dt = jnp.concatenate([dt, jnp.zeros((pad, LANES), F32)], axis=0)
            z = jnp.concatenate([z, jnp.zeros((pad, d_in), F32)], axis=0)
        x = xbc[:, 0:d_in]
        cs = _dot_exact_lhs(tri, dt * a_neg)
        cs_t = cs.T
        dt_e = _dot_exact_rhs(dt, expand)
        cs_e = _dot_exact_rhs(cs, expand)
        xdt = x * dt_e
        xdt_b = xdt.astype(BF16)
        grow = jnp.exp(cs_e)
        cs_last = cs_e[blk - 1:blk, :]
        to_end = (jnp.exp(cs_last - cs_e) * xdt).astype(BF16)
        carry_decay = jnp.exp(cs_last)
        ys = []
        for g in range(SSM_GROUPS):
            bmat = xbc[:, d_in + g * SSM_D_STATE:d_in + (g + 1) * SSM_D_STATE]
            cmat = xbc[:, d_in + (SSM_GROUPS + g) * SSM_D_STATE:d_in + (SSM_GROUPS + g + 1) * SSM_D_STATE]
            cmat_b = cmat.astype(BF16)
            cb = _dot_nt(cmat_b, bmat.astype(BF16))
            hprev = ht[g]
            y_g = _dot(cmat_b, hprev.astype(BF16)) * grow[:, g * gw:(g + 1) * gw]
            parts = []
            for pr in range(gw // LANES):
                l0 = g * gw + pr * LANES
                slab = xdt_b[:, l0:l0 + LANES]
                hd = l0 // SSM_HEAD_DIM
                res = []
                for hh in (hd, hd + 1):
                    decay = jnp.exp(jnp.where(tril, cs[:, hh:hh + 1] - cs_t[hh:hh + 1, :], -jnp.inf))
                    res.append(_dot((cb * decay).astype(BF16), slab))
                parts.append(jnp.where(lane < SSM_HEAD_DIM, res[0], res[1]))
            y_g = y_g + jnp.concatenate(parts, axis=1)
            ht[g] = carry_decay[:, g * gw:(g + 1) * gw] * hprev + _dot(bmat.T.astype(BF16), to_end[:, g * gw:(g + 1) * gw])
            ys.append(y_g)
        y = jnp.concatenate(ys, axis=1)
        y = (y + dexp_ref[...] * x) * _silu(z)
        outs = []
        for g in range(SSM_GROUPS):
            outs.append(_rms(y[:, g * gw:(g + 1) * gw], gn_ref[:, g * gw:(g + 1) * gw]))
        y = jnp.concatenate(outs, axis=1)
        y_ref[0, pl.ds(t0, rows), :] = y[0:rows].astype(y_ref.dtype)

    def full_block(i, carry):
        block(pl.multiple_of(i * blk, blk), blk)
        return carry

    lax.fori_loop(0, n_full, full_block, 0)
    if tail:
        block(n_full * blk, tail)
    for g in range(SSM_GROUPS):
        for j in range(gw // LANES):
            hfin_ref[0, g * gw + j * LANES:g * gw + (j + 1) * LANES, :] = ht[g, :, j * LANES:(j + 1) * LANES].T


def _ssd(xbc, z, dt, hist8, h0, cw, cb, dtb, alog, dexp, gn):
    b, seq, cdim = xbc.shape
    d_in = SSM_HEADS * SSM_HEAD_DIM
    assert seq % SUBLANES == 0 and (seq % SSD_BLOCK) % BF16_ROWS == 0
    per_b = lambda n: pl.BlockSpec((1, seq, n), lambda i: (i, 0, 0))
    const = lambda shape: pl.BlockSpec(shape, lambda i: (0,) * len(shape))
    return pl.pallas_call(
        functools.partial(_ssd_kernel, seq=seq),
        grid=(b,),
        in_specs=[per_b(cdim), per_b(d_in), per_b(LANES),
                  pl.BlockSpec((1, SUBLANES, cdim), lambda i: (i, 0, 0)),
                  pl.BlockSpec((1, d_in, SSM_D_STATE), lambda i: (i, 0, 0)),
                  const((SSM_CONV, cdim)), const((1, cdim)), const((1, LANES)), const((1, LANES)),
                  const((1, d_in)), const((1, d_in))],
        out_specs=[per_b(d_in), pl.BlockSpec((1, d_in, SSM_D_STATE), lambda i: (i, 0, 0))],
        out_shape=[jax.ShapeDtypeStruct((b, seq, d_in), BF16),
                   jax.ShapeDtypeStruct((b, d_in, SSM_D_STATE), F32)],
        scratch_shapes=[pltpu.VMEM((SUBLANES, cdim), F32),
                        pltpu.VMEM((SSM_GROUPS, SSM_D_STATE, d_in // SSM_GROUPS), F32)],
        compiler_params=_params(("arbitrary",)),
        name="ssd",
    )(xbc, z, dt, hist8, h0, cw, cb, dtb, alog, dexp, gn)


def _out_ffn_kernel(x_ref, da_ref, y_ref, sb_ref, wo_ref, g_ref, wg_ref, wu_ref, wd_ref,
                    fw_ref, fb_ref, init_ref, o_ref, last_ref, act_ref):
    ts = x_ref.shape[1]
    d_ff = wg_ref.shape[1]
    nd = da_ref.shape[2]
    ny = y_ref.shape[2]

    @pl.when(pl.program_id(1) == 0)
    def _():
        last_ref[...] = init_ref[...]

    xr = (x_ref[0] + _dot(da_ref[0], wo_ref[0:nd, :]) + _dot(y_ref[0], wo_ref[nd:nd + ny, :])
          + _dot(sb_ref[0], wo_ref[nd + ny:, :]))
    h2 = _rms(xr, g_ref[...]).astype(BF16)
    row8 = lax.broadcasted_iota(jnp.int32, (SUBLANES, FF_CHUNK), 0)
    for c0 in range(0, d_ff, FF_CHUNK):
        cols = slice(c0, c0 + FF_CHUNK)
        g = _dot(h2, wg_ref[:, cols])
        u = _dot(h2, wu_ref[:, cols])
        hist = last_ref[0, :, cols]
        conv = fw_ref[2:3, cols] * g + fb_ref[:, cols]
        for k in (1, 2):
            shifted = pltpu.roll(g, k, axis=0)
            top = jnp.where(row8 < k, pltpu.roll(hist, k, axis=0), shifted[0:SUBLANES])
            shifted = jnp.concatenate([top, shifted[SUBLANES:]], axis=0)
            conv = conv + fw_ref[2 - k:3 - k, cols] * shifted
        last_ref[0, :, cols] = g[ts - SUBLANES:ts]
        act_ref[:, cols] = (_silu(conv) * u).astype(BF16)
    o_ref[0] = xr + _dot(act_ref[...], wd_ref[...])


def _out_ffn(x, da_o, y, sb_o, wo, g, wg, wu, wd, fw, fb, init8):
    b, seq, d = x.shape
    d_ff = wg.shape[1]
    assert d_ff % FF_CHUNK == 0
    ts = _pick_tile(seq, 704, BF16_ROWS)
    row = lambda n: pl.BlockSpec((1, ts, n), lambda i, j: (i, j, 0))
    state = lambda: pl.BlockSpec((1, SUBLANES, d_ff), lambda i, j: (i, 0, 0))
    return pl.pallas_call(
        _out_ffn_kernel,
        grid=(b, seq // ts),
        in_specs=[row(d), row(da_o.shape[2]), row(y.shape[2]), row(sb_o.shape[2]),
                  _resident(wo.shape), _resident((1, d)), _resident(wg.shape), _resident(wu.shape),
                  _resident(wd.shape), _resident(fw.shape), _resident((1, d_ff)), state()],
        out_specs=[row(d), state()],
        out_shape=[jax.ShapeDtypeStruct((b, seq, d), F32),
                   jax.ShapeDtypeStruct((b, SUBLANES, d_ff), F32)],
        scratch_shapes=[pltpu.VMEM((ts, d_ff), BF16)],
        compiler_params=_params(("arbitrary", "arbitrary")),
        name="out_ffn",
    )(x, da_o, y, sb_o, wo, g, wg, wu, wd, fw, fb, init8)


def _final_kernel(x_ref, g_ref, o_ref, *, skip):
    n = o_ref.shape[1]
    o_ref[0] = _rms(x_ref[0, skip:skip + n, :], g_ref[...])


def _final_norm(x, g, skip):
    b, seq, d = x.shape
    n = seq - skip
    assert skip % SUBLANES == 0
    return pl.pallas_call(
        functools.partial(_final_kernel, skip=skip),
        grid=(b,),
        in_specs=[pl.BlockSpec((1, seq, d), lambda i: (i, 0, 0)), pl.BlockSpec((1, d), lambda i: (0, 0))],
        out_specs=pl.BlockSpec((1, n, d), lambda i: (i, 0, 0)),
        out_shape=jax.ShapeDtypeStruct((b, n, d), F32),
        compiler_params=_params(("arbitrary",)),
        name="final_norm",
    )(x, g)


def _pad_rows_top(a, rows):
    return jnp.pad(a, ((0, 0), (rows - a.shape[1], 0), (0, 0)))


def _layer(x, past, layer, p, lam_init, bias_prompt, bias_sample):
    b, seq, d = x.shape
    proj = _in_proj(x.reshape(b * seq, d), p["norm_mix"], p["w_in"])
    da_q, da_k, da_v, ssm_z, ssm_xbc, sb_q, sb_k, sb_v, ssm_dt = [a.reshape(b, seq, -1) for a in proj]
    if past is None:
        da_o = _da_prompt(da_q, da_k, da_v, bias_prompt, p["lamp"], p["da_subln"], lam_init)
        sb_o = _sb_prompt(sb_q, sb_k, sb_v)
        hist8 = jnp.zeros((b, SUBLANES, ssm_xbc.shape[2]), F32)
        h0 = jnp.zeros((b, SSM_HEADS * SSM_HEAD_DIM, SSM_D_STATE), F32)
        ffn_init = jnp.zeros((b, SUBLANES, p["w_gate"].shape[1]), F32)
    else:
        c_da_k, c_da_v, c_sb_k, c_sb_v, state_ssm, state_conv, state_ffn = past
        da_o = _da_sample(da_q, da_k, da_v, c_da_k, c_da_v, layer, bias_sample, p["lamp"], p["da_subln"], lam_init)
        sb_o = _sb_sample(sb_q, sb_k, sb_v, c_sb_k, c_sb_v, layer)
        hist8 = _pad_rows_top(state_conv[layer], SUBLANES)
        h0 = state_ssm[layer].reshape(b, SSM_HEADS * SSM_HEAD_DIM, SSM_D_STATE)
        ffn_init = _pad_rows_top(state_ffn[layer], SUBLANES)
    y, h_final = _ssd(ssm_xbc, ssm_z, ssm_dt, hist8, h0, p["conv_w"], p["conv_b"], p["dt_bias"],
                      p["a_log"], p["d_exp"], p["ssm_norm"])
    x_new, ffn_last = _out_ffn(x, da_o, y, sb_o, p["w_out"], p["norm_ffn"], p["w_gate"], p["w_up"],
                               p["w_down"], p["fconv_w"], p["fconv_b"], ffn_init)
    conv_in_tail = jnp.concatenate([hist8, ssm_xbc[:, -min(seq, SUBLANES):]], axis=1)
    states = (da_k.reshape(b, seq, DA_HEADS, 2, DA_HEAD_DIM),
              da_v.reshape(b, seq, DA_HEADS, DA_VDIM),
              sb_k.reshape(b, seq, SB_HEADS, SB_HEAD_DIM),
              sb_v.reshape(b, seq, SB_HEADS, SB_HEAD_DIM),
              h_final.reshape(b, SSM_HEADS, SSM_HEAD_DIM, SSM_D_STATE),
              conv_in_tail[:, -(SSM_CONV - 1):],
              ffn_last[:, -(FFN_CONV - 1):])
    return x_new, states


def _pad_lanes(v, n):
    return jnp.pad(v, (0, n - v.shape[0])).reshape(1, n)


def kernel(x_prompt, x_sample, cache_da_k, cache_da_v, cache_sb_k, cache_sb_v, state_ssm, state_ssm_conv, state_ffn_conv, meta_tokens, rel_bias_table, w_in, w_out, norm_mix, norm_ffn, da_lambda_q1, da_lambda_k1, da_lambda_q2, da_lambda_k2, da_subln, ssm_conv_w, ssm_conv_b, ssm_dt_bias, ssm_a_log, ssm_d, ssm_norm, ffn_w_gate, ffn_w_up, ffn_w_down, ffn_conv_w, ffn_conv_b, final_norm):
    depth = w_in.shape[0]
    b = x_prompt.shape[0]
    d = x_prompt.shape[2]
    sb, n_new, _ = x_sample.shape
    past_len = cache_da_k.shape[2]
    n_meta = meta_tokens.shape[0]
    assert n_meta == N_META

    xp = jnp.concatenate([jnp.broadcast_to(meta_tokens[None], (b, n_meta, d)), x_prompt], axis=1)
    xs = x_sample
    bias_prompt, bias_sample = _bias_tiles(rel_bias_table, past_len, n_new)
    cache = (cache_da_k.reshape(depth, sb, past_len, -1), cache_da_v.reshape(depth, sb, past_len, -1),
             cache_sb_k.reshape(depth, sb, past_len, -1), cache_sb_v.reshape(depth, sb, past_len, -1),
             state_ssm, state_ssm_conv, state_ffn_conv)

    p_states, s_states = [], []
    for i in range(depth):
        wi = w_in[i]
        w_r = jnp.concatenate([wi[:, :_OFF_DT], wi[:, _OFF_DT + SSM_HEADS:],
                               jnp.pad(wi[:, _OFF_DT:_OFF_DT + SSM_HEADS], ((0, 0), (0, LANES - SSM_HEADS)))],
                              axis=1).astype(BF16)
        p = dict(
            w_in=w_r, w_out=w_out[i].astype(BF16),
            norm_mix=norm_mix[i].reshape(1, d), norm_ffn=norm_ffn[i].reshape(1, d),
            lamp=jnp.stack([da_lambda_q1[i], da_lambda_k1[i], da_lambda_q2[i], da_lambda_k2[i]]),
            da_subln=da_subln[i].reshape(1, DA_VDIM),
            conv_w=ssm_conv_w[i], conv_b=ssm_conv_b[i].reshape(1, -1),
            dt_bias=_pad_lanes(ssm_dt_bias[i], LANES), a_log=_pad_lanes(ssm_a_log[i], LANES),
            d_exp=jnp.repeat(ssm_d[i], SSM_HEAD_DIM).reshape(1, -1), ssm_norm=ssm_norm[i].reshape(1, -1),
            w_gate=ffn_w_gate[i].astype(BF16), w_up=ffn_w_up[i].astype(BF16), w_down=ffn_w_down[i].astype(BF16),
            fconv_w=ffn_conv_w[i], fconv_b=ffn_conv_b[i].reshape(1, -1),
        )
        lam_init = 0.8 - 0.6 * math.exp(-0.3 * i)
        xp, st_p = _layer(xp, None, i, p, lam_init, bias_prompt, bias_sample)
        xs, st_s = _layer(xs, cache, i, p, lam_init, bias_prompt, bias_sample)
        p_states.append(st_p)
        s_states.append(st_s)

    g_fin = final_norm.reshape(1, d)
    y_prompt = _final_norm(xp, g_fin, n_meta)
    y_sample = _final_norm(xs, g_fin, 0)
    stack = lambda states: tuple(jnp.stack([s[j] for s in states], axis=0) for j in range(7))
    return (y_prompt, y_sample) + stack(p_states) + stack(s_states)
```

```python
import functools
import math

import jax
import jax.numpy as jnp
from jax import lax
from jax.experimental import pallas as pl
from jax.experimental.pallas import tpu as pltpu

F32 = jnp.float32
BF16 = jnp.bfloat16

LANES = 128
SUBLANES = 8
BF16_ROWS = 16
V7X_VMEM_BYTES = 64 * 1024 * 1024
VMEM_LIMIT = 56 * 1024 * 1024

CHUNK = 64
N_META = 16
EPS = 1e-6
NEG_INF = -1e30
DA_HEADS = 4
DA_HEAD_DIM = 64
DA_VDIM = 128
SB_HEADS = 8
SB_HEAD_DIM = 64
SSM_HEADS = 8
SSM_HEAD_DIM = 64
SSM_GROUPS = 2
SSM_D_STATE = 128
SSM_CONV = 4
FFN_CONV = 3
REL_BUCKETS = 32
REL_MAX_DIST = 128
QK_SCALE = 0.125
SSD_BLOCK = 128
FF_CHUNK = 256
KEY_GROUP = 4
GROUP_KEYS = KEY_GROUP * LANES
QUERY_GROUP = KEY_GROUP
SUPER_ROWS = QUERY_GROUP * LANES
BIAS_HIDDEN = 4


def _dot(a, b):
    return jnp.dot(a, b, preferred_element_type=F32)


def _dot_nt(a, b):
    return lax.dot_general(a, b, (((1,), (1,)), ((), ())), preferred_element_type=F32)


def _split3(x):
    hi = x.astype(BF16)
    r1 = x - hi.astype(F32)
    mid = r1.astype(BF16)
    lo = (r1 - mid.astype(F32)).astype(BF16)
    return hi, mid, lo


def _dot_exact_rhs(x, m_bf16):
    hi, mid, lo = _split3(x)
    return _dot(hi, m_bf16) + _dot(mid, m_bf16) + _dot(lo, m_bf16)


def _dot_exact_lhs(m_bf16, x):
    hi, mid, lo = _split3(x)
    return _dot(m_bf16, hi) + _dot(m_bf16, mid) + _dot(m_bf16, lo)


def _rms(x, g):
    return x * lax.rsqrt(jnp.mean(x * x, axis=-1, keepdims=True) + EPS) * g


def _silu(x):
    return x * (1.0 / (1.0 + jnp.exp(-x)))


def _softplus(x):
    return jnp.maximum(x, 0.0) + jnp.log1p(jnp.exp(-jnp.abs(x)))


def _pick_tile(n, cap, mult):
    best = None
    for t in range(mult, min(n, cap) + 1, mult):
        if n % t == 0:
            best = t
    assert best is not None, (n, cap, mult)
    return best


def _resident(shape):
    nd = len(shape)
    return pl.BlockSpec(shape, lambda *_: (0,) * nd, pipeline_mode=pl.Buffered(1))


def _params(sem):
    return pltpu.CompilerParams(dimension_semantics=sem, vmem_limit_bytes=VMEM_LIMIT)


def _rel_bucket(rel):
    half = REL_BUCKETS // 2
    max_exact = half // 2
    n = jnp.abs(rel)
    nf = jnp.maximum(n, 1).astype(jnp.float32)
    large = max_exact + (jnp.log(nf / max_exact) / math.log(REL_MAX_DIST / max_exact)
                         * (half - max_exact)).astype(jnp.int32)
    large = jnp.minimum(large, half - 1)
    return jnp.where(rel > 0, half, 0) + jnp.where(n < max_exact, n, large)


def _bias_kernel(tab_ref, bp_ref, bs_ref, op_ref, os_ref, *, past, n_new):
    h = pl.program_id(0)

    def lookup(bucket):
        val = jnp.zeros(bucket.shape, F32)
        for b in range(REL_BUCKETS):
            val = jnp.where(bucket == b, tab_ref[b, h], val)
        return val

    r = lax.broadcasted_iota(jnp.int32, (LANES, LANES), 0)
    c = lax.broadcasted_iota(jnp.int32, (LANES, LANES), 1)
    qc = jnp.right_shift(r - N_META, 6)
    kc = jnp.right_shift(c - N_META, 6)
    for slot in range(BIAS_HIDDEN):
        val = lookup(bp_ref[slot])
        if slot > 0:
            d = slot - 2
            val = jnp.where(2 * d + kc - qc <= 0, val, NEG_INF)
        op_ref[0, slot] = val
    op_ref[0, BIAS_HIDDEN] = jnp.full((LANES, LANES), NEG_INF, F32)

    lw = bs_ref.shape[1]
    rs = lax.broadcasted_iota(jnp.int32, (n_new, lw), 0) + past
    cs = lax.broadcasted_iota(jnp.int32, (n_new, lw), 1)
    vis = (jnp.right_shift(cs, 6) <= jnp.right_shift(rs, 6)) & (cs < past + n_new)
    os_ref[0] = jnp.where(vis, lookup(bs_ref[...]), NEG_INF)


def _bias_tiles(rel_table, past, n_new):
    lw = pl.cdiv(past + n_new, LANES) * LANES
    r = jnp.arange(LANES, dtype=jnp.int32)[:, None]
    c = jnp.arange(LANES, dtype=jnp.int32)[None, :]
    far = jnp.full((LANES, LANES), -2 * LANES, jnp.int32)
    bp = jnp.stack([_rel_bucket(far)] + [_rel_bucket(d * LANES + c - r) for d in (-1, 0, 1)])
    qs = past + jnp.arange(n_new, dtype=jnp.int32)[:, None]
    ks = jnp.arange(lw, dtype=jnp.int32)[None, :]
    bs = _rel_bucket(ks - qs)
    return pl.pallas_call(
        functools.partial(_bias_kernel, past=past, n_new=n_new),
        grid=(DA_HEADS,),
        in_specs=[pl.BlockSpec(memory_space=pltpu.SMEM),
                  pl.BlockSpec((4, LANES, LANES), lambda h: (0, 0, 0)),
                  pl.BlockSpec((n_new, lw), lambda h: (0, 0))],
        out_specs=[pl.BlockSpec((1, BIAS_HIDDEN + 1, LANES, LANES), lambda h: (h, 0, 0, 0)),
                   pl.BlockSpec((1, n_new, lw), lambda h: (h, 0, 0))],
        out_shape=[jax.ShapeDtypeStruct((DA_HEADS, BIAS_HIDDEN + 1, LANES, LANES), F32),
                   jax.ShapeDtypeStruct((DA_HEADS, n_new, lw), F32)],
        name="rel_bias_tiles",
    )(rel_table, bp, bs)


_SEG = (("da_q", 512, BF16), ("da_k", 512, F32), ("da_v", 512, F32), ("ssm_z", 512, F32),
        ("ssm_xbc", 1024, F32), ("sb_q", 512, BF16), ("sb_k", 512, F32), ("sb_v", 512, F32),
        ("ssm_dt", LANES, F32))
_N_PROJ = sum(s[1] for s in _SEG)
_OFF_DT = 3072


_STATE_SEGS = ("da_k", "da_v", "sb_k", "sb_v")


def _inproj_kernel(x_ref, g_ref, w_ref, *refs):
    out_refs = refs[len(refs) - len(_SEG):]
    h = _rms(x_ref[...], g_ref[...]).astype(BF16)
    off = 0
    for (_, n, dt), o_ref in zip(_SEG, out_refs):
        o_ref[...] = _dot(h, w_ref[:, off:off + n]).astype(dt)
        off += n


def _in_proj(x2d, g, w_r, layer, depth, stacks):
    t, d = x2d.shape
    tm = _pick_tile(t, 512, BF16_ROWS)
    in_specs = [pl.BlockSpec((tm, d), lambda i: (i, 0)), _resident((1, d)), _resident((d, _N_PROJ))]
    args = [x2d, g, w_r]
    out_specs, out_shape, aliases = [], [], {}
    for j, (name, n, dt) in enumerate(_SEG):
        if name in _STATE_SEGS:
            out_specs.append(pl.BlockSpec((None, tm, n), lambda i: (layer, i, 0)))
            out_shape.append(jax.ShapeDtypeStruct((depth, t, n), dt))
            if stacks is not None:
                aliases[len(args)] = j
                in_specs.append(pl.BlockSpec(memory_space=pl.ANY))
                args.append(stacks[_STATE_SEGS.index(name)])
        else:
            out_specs.append(pl.BlockSpec((tm, n), lambda i: (i, 0)))
            out_shape.append(jax.ShapeDtypeStruct((t, n), dt))
    return pl.pallas_call(
        _inproj_kernel,
        grid=(t // tm,),
        in_specs=in_specs,
        out_specs=out_specs,
        out_shape=out_shape,
        input_output_aliases=aliases,
        compiler_params=_params(("arbitrary",)),
        name="in_proj",
    )(*args)


def _lambda(lamp_ref, lam_init):
    lp = lamp_ref[...]
    return (jnp.exp(jnp.sum(lp[0:1] * lp[1:2], axis=1, keepdims=True))
            - jnp.exp(jnp.sum(lp[2:3] * lp[3:4], axis=1, keepdims=True)) + lam_init)


def _split_maps(q_bf16):
    lane = lax.broadcasted_iota(jnp.int32, (1, LANES), 1)
    qf = q_bf16.astype(F32) * QK_SCALE
    return (jnp.where(lane < DA_HEAD_DIM, qf, 0.0).astype(BF16),
            jnp.where(lane >= DA_HEAD_DIM, qf, 0.0).astype(BF16))


def _da_prompt_kernel(q_ref, k_ref, v_ref, bm_ref, lamp_ref, subln_ref, o_ref,
                      q1s, q2s, ks, vs, sbuf, sbig, mx_ref, l_ref, acc_ref, *, seq, lam_init):
    n_full, tail = divmod(seq, LANES)
    n_blk = n_full + (1 if tail else 0)
    lp = ks.shape[0]
    q1, q2 = _split_maps(q_ref[0])
    q1s[...] = q1
    q2s[...] = q2
    ks[0:seq, :] = k_ref[0].astype(BF16)
    vs[0:seq, :] = v_ref[0].astype(BF16)
    ks[seq:lp, :] = jnp.zeros((lp - seq, LANES), BF16)
    vs[seq:lp, :] = jnp.zeros((lp - seq, LANES), BF16)
    lam = _lambda(lamp_ref, lam_init)
    gsub = subln_ref[...] * (1.0 - lam_init)

    def tile(i, rows, r0):
        q1t = q1s[pl.ds(r0, rows), :]
        q2t = q2s[pl.ds(r0, rows), :]
        n_grp = (jnp.minimum(i + 2, n_blk) + KEY_GROUP - 1) // KEY_GROUP

        def scores(grp, carry):
            m1, m2 = carry
            base = pl.multiple_of(grp * GROUP_KEYS, GROUP_KEYS)
            for g in range(KEY_GROUP):
                kb = grp * KEY_GROUP + g
                kblk = ks[pl.ds(base + g * LANES, LANES), :]
                slot = jnp.where(kb >= n_blk, BIAS_HIDDEN, jnp.clip(kb - i + 2, 0, BIAS_HIDDEN))
                bias = bm_ref[0, slot, 0:rows, :]
                s1 = _dot_nt(q1t, kblk) + bias
                s2 = _dot_nt(q2t, kblk) + bias
                sbuf[0, grp, 0:rows, g * LANES:(g + 1) * LANES] = s1
                sbuf[1, grp, 0:rows, g * LANES:(g + 1) * LANES] = s2
                m1 = jnp.maximum(m1, s1)
                m2 = jnp.maximum(m2, s2)
            return m1, m2

        ninf = jnp.full((rows, LANES), -jnp.inf, F32)
        m1, m2 = lax.fori_loop(0, n_grp, scores, (ninf, ninf))
        m1 = jnp.max(m1, axis=1, keepdims=True)
        m2 = jnp.max(m2, axis=1, keepdims=True)

        def exp_pv(grp, carry):
            l1, l2, a1, a2 = carry
            base = pl.multiple_of(grp * GROUP_KEYS, GROUP_KEYS)
            ps1, ps2 = [], []
            for g in range(KEY_GROUP):
                p1 = jnp.exp(sbuf[0, grp, 0:rows, g * LANES:(g + 1) * LANES] - m1)
                p2 = jnp.exp(sbuf[1, grp, 0:rows, g * LANES:(g + 1) * LANES] - m2)
                l1 = l1 + p1
                l2 = l2 + p2
                ps1.append(p1.astype(BF16))
                ps2.append(p2.astype(BF16))
            vrows = vs[pl.ds(base, GROUP_KEYS), :]
            a1 = a1 + _dot(jnp.concatenate(ps1, axis=1), vrows)
            a2 = a2 + _dot(jnp.concatenate(ps2, axis=1), vrows)
            return l1, l2, a1, a2

        zero = jnp.zeros((rows, LANES), F32)
        l1, l2, a1, a2 = lax.fori_loop(0, n_grp, exp_pv, (zero, zero, zero, zero))
        c1 = 1.0 / jnp.sum(l1, axis=1, keepdims=True)
        c2 = lam / jnp.sum(l2, axis=1, keepdims=True)
        o_ref[0, pl.ds(r0, rows), :] = _rms(a1 * c1 - a2 * c2, gsub).astype(o_ref.dtype)

    def super_tile(a, carry):
        r_base = pl.multiple_of(a * SUPER_ROWS, SUPER_ROWS)
        k_base = pl.multiple_of(a * GROUP_KEYS, GROUP_KEYS)
        maps = (q1s, q2s)
        ninf = jnp.full((LANES, LANES), -jnp.inf, F32)
        zero = jnp.zeros((LANES, LANES), F32)
        for m in range(2):
            for tq in range(QUERY_GROUP):
                mx_ref[m, tq] = ninf
                l_ref[m, tq] = zero
                acc_ref[m, tq] = zero

        def scores_lower(grp, c):
            base = pl.multiple_of(grp * GROUP_KEYS, GROUP_KEYS)
            for m in range(2):
                qsup = maps[m][pl.ds(r_base, SUPER_ROWS), :]
                mx = [mx_ref[m, tq] for tq in range(QUERY_GROUP)]
                for g in range(KEY_GROUP):
                    kb = grp * KEY_GROUP + g
                    s_all = _dot_nt(qsup, ks[pl.ds(base + g * LANES, LANES), :])
                    for tq in range(QUERY_GROUP):
                        slot = jnp.clip(kb - (a * QUERY_GROUP + tq) + 2, 0, BIAS_HIDDEN)
                        s = s_all[tq * LANES:(tq + 1) * LANES] + bm_ref[0, slot]
                        sbig[m, tq, kb] = s
                        mx[tq] = jnp.maximum(mx[tq], s)
                for tq in range(QUERY_GROUP):
                    mx_ref[m, tq] = mx[tq]
            return c

        lax.fori_loop(0, a, scores_lower, 0)

        for m in range(2):
            qsup = maps[m][pl.ds(r_base, SUPER_ROWS), :]
            mx = [mx_ref[m, tq] for tq in range(QUERY_GROUP)]
            for g in range(KEY_GROUP + 1):
                first = max(g - 1, 0)
                kb = a * KEY_GROUP + g
                s_all = _dot_nt(qsup[first * LANES:, :], ks[pl.ds(k_base + g * LANES, LANES), :])
                for tq in range(first, QUERY_GROUP):
                    slot = 0 if g - tq <= -2 else g - tq + 2
                    if g == KEY_GROUP:
                        slot = jnp.where(kb >= n_blk, BIAS_HIDDEN, slot)
                    s = s_all[(tq - first) * LANES:(tq - first + 1) * LANES] + bm_ref[0, slot]
                    sbig[m, tq, kb] = s
                    mx[tq] = jnp.maximum(mx[tq], s)
            for tq in range(QUERY_GROUP):
                mx_ref[m, tq] = jnp.broadcast_to(jnp.max(mx[tq], axis=1, keepdims=True), (LANES, LANES))

        def pv_lower(grp, c):
            base = pl.multiple_of(grp * GROUP_KEYS, GROUP_KEYS)
            vrows = vs[pl.ds(base, GROUP_KEYS), :]
            for m in range(2):
                prows = []
                for tq in range(QUERY_GROUP):
                    mrow = mx_ref[m, tq]
                    lsum = l_ref[m, tq]
                    ps = []
                    for g in range(KEY_GROUP):
                        p = jnp.exp(sbig[m, tq, grp * KEY_GROUP + g] - mrow)
                        lsum = lsum + p
                        ps.append(p.astype(BF16))
                    l_ref[m, tq] = lsum
                    prows.append(jnp.concatenate(ps, axis=1))
                pv = _dot(jnp.concatenate(prows, axis=0), vrows)
                for tq in range(QUERY_GROUP):
                    acc_ref[m, tq] += pv[tq * LANES:(tq + 1) * LANES]
            return c

        lax.fori_loop(0, a, pv_lower, 0)

        for tq in range(QUERY_GROUP):
            nb = tq + 2
            vrows = vs[pl.ds(k_base, nb * LANES), :]
            outs = []
            for m in range(2):
                mrow = mx_ref[m, tq]
                lsum = l_ref[m, tq]
                ps = []
                for g in range(nb):
                    p = jnp.exp(sbig[m, tq, a * KEY_GROUP + g] - mrow)
                    lsum = lsum + p
                    ps.append(p.astype(BF16))
                acc = acc_ref[m, tq] + _dot(jnp.concatenate(ps, axis=1), vrows)
                outs.append((acc, jnp.sum(lsum, axis=1, keepdims=True)))
            o = outs[0][0] * (1.0 / outs[0][1]) - outs[1][0] * (lam / outs[1][1])
            o_ref[0, pl.ds(r_base + tq * LANES, LANES), :] = _rms(o, gsub).astype(o_ref.dtype)
        return carry

    def full_tile(i, carry):
        tile(i, LANES, pl.multiple_of(i * LANES, LANES))
        return carry

    n_super = n_full // QUERY_GROUP
    lax.fori_loop(0, n_super, super_tile, 0)
    lax.fori_loop(n_super * QUERY_GROUP, n_full, full_tile, 0)
    if tail:
        r0 = n_full * LANES
        nk = n_blk * LANES
        tiles = [bm_ref[0, 0, 0:tail, :]] * max(n_full - 1, 0)
        if n_full >= 1:
            tiles.append(bm_ref[0, 1, 0:tail, :])
        tiles.append(bm_ref[0, 2, 0:tail, :])
        bias = tiles[0] if len(tiles) == 1 else jnp.concatenate(tiles, axis=1)
        kall = ks[0:nk, :]
        s1 = _dot_nt(q1s[r0:r0 + tail, :], kall) + bias
        s2 = _dot_nt(q2s[r0:r0 + tail, :], kall) + bias
        p1 = jnp.exp(s1 - jnp.max(s1, axis=1, keepdims=True))
        p2 = jnp.exp(s2 - jnp.max(s2, axis=1, keepdims=True))
        c1 = 1.0 / jnp.sum(p1, axis=1, keepdims=True)
        c2 = lam / jnp.sum(p2, axis=1, keepdims=True)
        acc = _dot((p1 * c1 - p2 * c2).astype(BF16), vs[0:nk, :])
        o_ref[0, r0:r0 + tail, :] = _rms(acc, gsub).astype(o_ref.dtype)


def _da_prompt(q, k, v, layer, bm, lamp, subln, lam_init):
    b, seq, _ = q.shape
    assert seq % BF16_ROWS == 0 and (seq - N_META) % CHUNK == 0
    n_blk = pl.cdiv(seq, LANES)
    n_grp = pl.cdiv(n_blk + 1, KEY_GROUP)
    lp = n_grp * GROUP_KEYS
    n_big = (seq // SUPER_ROWS) * QUERY_GROUP + 1
    stat = lambda: pltpu.VMEM((2, QUERY_GROUP, LANES, LANES), F32)
    blk = lambda: pl.BlockSpec((1, seq, LANES), lambda i, h: (i, 0, h))
    stk = lambda: pl.BlockSpec((None, 1, seq, LANES), lambda i, h: (layer, i, 0, h))
    return pl.pallas_call(
        functools.partial(_da_prompt_kernel, seq=seq, lam_init=lam_init),
        grid=(b, DA_HEADS),
        in_specs=[blk(), stk(), stk(),
                  pl.BlockSpec((1, BIAS_HIDDEN + 1, LANES, LANES), lambda i, h: (h, 0, 0, 0)),
                  pl.BlockSpec((4, DA_HEAD_DIM), lambda i, h: (0, 0)),
                  pl.BlockSpec((1, DA_VDIM), lambda i, h: (0, 0))],
        out_specs=blk(),
        out_shape=jax.ShapeDtypeStruct((b, seq, DA_HEADS * DA_VDIM), BF16),
        scratch_shapes=[pltpu.VMEM((seq, LANES), BF16), pltpu.VMEM((seq, LANES), BF16),
                        pltpu.VMEM((lp, LANES), BF16), pltpu.VMEM((lp, LANES), BF16),
                        pltpu.VMEM((2, n_grp, LANES, GROUP_KEYS), F32),
                        pltpu.VMEM((2, QUERY_GROUP, n_big, LANES, LANES), F32), stat(), stat(), stat()],
        compiler_params=_params(("arbitrary", "arbitrary")),
        name="diff_attn_prompt",
    )(q, k, v, bm, lamp, subln)


def _da_sample_kernel(q_ref, k_ref, v_ref, ck_ref, cv_ref, bs_ref, lamp_ref, subln_ref, o_ref,
                      ks, vs, *, past, n_new, lam_init):
    lw = ks.shape[0]
    tot = past + n_new
    ks[0:past, :] = ck_ref[0, 0].astype(BF16)
    vs[0:past, :] = cv_ref[0, 0].astype(BF16)
    ks[past:tot, :] = k_ref[0].astype(BF16)
    vs[past:tot, :] = v_ref[0].astype(BF16)
    if lw > tot:
        ks[tot:lw, :] = jnp.zeros((lw - tot, LANES), BF16)
        vs[tot:lw, :] = jnp.zeros((lw - tot, LANES), BF16)
    q1, q2 = _split_maps(q_ref[0])
    bias = bs_ref[0]
    kall = ks[...]
    s1 = _dot_nt(q1, kall) + bias
    s2 = _dot_nt(q2, kall) + bias
    p1 = jnp.exp(s1 - jnp.max(s1, axis=1, keepdims=True))
    p2 = jnp.exp(s2 - jnp.max(s2, axis=1, keepdims=True))
    c1 = 1.0 / jnp.sum(p1, axis=1, keepdims=True)
    c2 = _lambda(lamp_ref, lam_init) / jnp.sum(p2, axis=1, keepdims=True)
    acc = _dot((p1 * c1 - p2 * c2).astype(BF16), vs[...])
    o_ref[0] = _rms(acc, subln_ref[...] * (1.0 - lam_init)).astype(o_ref.dtype)


def _da_sample(q, k, v, cache_k, cache_v, layer, bs, lamp, subln, lam_init):
    b, n_new, _ = q.shape
    past = cache_k.shape[2]
    assert past % BF16_ROWS == 0 and n_new % BF16_ROWS == 0
    lw = bs.shape[2]
    new = lambda: pl.BlockSpec((1, n_new, LANES), lambda i, h: (i, 0, h))
    stk = lambda: pl.BlockSpec((None, 1, n_new, LANES), lambda i, h: (layer, i, 0, h))
    old = lambda: pl.BlockSpec((1, 1, past, LANES), lambda i, h: (layer, i, 0, h))
    return pl.pallas_call(
        functools.partial(_da_sample_kernel, past=past, n_new=n_new, lam_init=lam_init),
        grid=(b, DA_HEADS),
        in_specs=[new(), stk(), stk(), old(), old(),
                  pl.BlockSpec((1, n_new, lw), lambda i, h: (h, 0, 0)),
                  pl.BlockSpec((4, DA_HEAD_DIM), lambda i, h: (0, 0)),
                  pl.BlockSpec((1, DA_VDIM), lambda i, h: (0, 0))],
        out_specs=new(),
        out_shape=jax.ShapeDtypeStruct((b, n_new, DA_HEADS * DA_VDIM), BF16),
        scratch_shapes=[pltpu.VMEM((lw, LANES), BF16), pltpu.VMEM((lw, LANES), BF16)],
        compiler_params=_params(("arbitrary", "arbitrary")),
        name="diff_attn_sample",
    )(q, k, v, cache_k, cache_v, bs, lamp, subln)


def _split_heads(q_bf16):
    lane = lax.broadcasted_iota(jnp.int32, (1, LANES), 1)
    qf = q_bf16.astype(F32) * QK_SCALE
    return (jnp.where(lane < SB_HEAD_DIM, qf, 0.0).astype(BF16),
            jnp.where(lane >= SB_HEAD_DIM, qf, 0.0).astype(BF16))


def _log_sigmoid(z):
    neg_abs = lax.bitcast_convert_type(lax.bitcast_convert_type(z, jnp.uint32) | jnp.uint32(0x80000000), F32)
    return jnp.minimum(z, 0.0) - jnp.log(1.0 + jnp.exp(neg_abs))


def _hi_lo(x):
    hi = lax.bitcast_convert_type(lax.bitcast_convert_type(x, jnp.uint32) & jnp.uint32(0xFFFF0000), F32)
    return jnp.concatenate([hi.astype(BF16), (x - hi).astype(BF16)], axis=1)


def _sb_tile(i, rows, qts, ks, vs, suffix):
    row = lax.broadcasted_iota(jnp.int32, (rows, LANES), 0)
    col = lax.broadcasted_iota(jnp.int32, (rows, LANES), 1)
    col_minus_row = col - row
    top = i // KEY_GROUP

    def group(grp, runs, accs, diagonal):
        base = pl.multiple_of(grp * GROUP_KEYS, GROUP_KEYS)
        vrows = vs[pl.ds(base, GROUP_KEYS), :]
        new_runs, new_accs = [], []
        for qt, run, acc in zip(qts, runs, accs):
            log_betas, inners, totals, visible = [], [], [], []
            for g in range(KEY_GROUP):
                z = _dot_nt(qt, ks[pl.ds(base + g * LANES, LANES), :])
                log_beta = _log_sigmoid(z)
                log_keep = log_beta - z
                if diagonal:
                    vis = col_minus_row < (i - grp * KEY_GROUP - g) * LANES
                    log_keep = jnp.where(vis, log_keep, 0.0)
                    visible.append(vis)
                both = _dot(_hi_lo(log_keep), suffix)
                log_betas.append(log_beta)
                inners.append(both[:, 0:LANES])
                totals.append(both[:, LANES:])
            ws = [None] * KEY_GROUP
            for g in reversed(range(KEY_GROUP)):
                w = jnp.exp(log_betas[g] + inners[g] + run)
                if diagonal:
                    w = jnp.where(visible[g], w, 0.0)
                ws[g] = w.astype(BF16)
                run = run + totals[g]
            new_runs.append(run)
            new_accs.append(acc + _dot(jnp.concatenate(ws, axis=1), vrows))
        return new_runs, new_accs

    zero = jnp.zeros((rows, LANES), F32)
    runs, accs = group(top, [zero, zero], [zero, zero], True)

    def body(t, carry):
        r, a = group(top - 1 - t, carry[0:2], carry[2:4], False)
        return (r[0], r[1], a[0], a[1])

    carry = lax.fori_loop(0, top, body, (runs[0], runs[1], accs[0], accs[1]))
    lane = lax.broadcasted_iota(jnp.int32, (1, LANES), 1)
    return jnp.where(lane < SB_HEAD_DIM, carry[2], carry[3])


def _sb_rows_static(i, rows, qts, ks, vs, suffix):
    nk = (i + 1) * LANES
    row = lax.broadcasted_iota(jnp.int32, (rows, LANES), 0)
    col = lax.broadcasted_iota(jnp.int32, (rows, LANES), 1)
    vis = col < row
    kall = ks[0:nk, :]
    vall = vs[0:nk, :]
    outs = []
    for qt in qts:
        z = _dot_nt(qt, kall)
        log_beta = _log_sigmoid(z)
        log_keep = log_beta - z
        sums = []
        for g in range(i + 1):
            lk = log_keep[:, g * LANES:(g + 1) * LANES]
            if g == i:
                lk = jnp.where(vis, lk, 0.0)
            sums.append(_dot(_hi_lo(lk), suffix))
        run = jnp.zeros((rows, LANES), F32)
        ws = [None] * (i + 1)
        for g in reversed(range(i + 1)):
            w = jnp.exp(log_beta[:, g * LANES:(g + 1) * LANES] + sums[g][:, 0:LANES] + run)
            if g == i:
                w = jnp.where(vis, w, 0.0)
            ws[g] = w.astype(BF16)
            run = run + sums[g][:, LANES:]
        outs.append(_dot(jnp.concatenate(ws, axis=1), vall))
    lane = lax.broadcasted_iota(jnp.int32, (1, LANES), 1)
    return jnp.where(lane < SB_HEAD_DIM, outs[0], outs[1])


def _suffix_matrix():
    r = lax.broadcasted_iota(jnp.int32, (2 * LANES, 2 * LANES), 0)
    c = lax.broadcasted_iota(jnp.int32, (2 * LANES, 2 * LANES), 1)
    later = jnp.bitwise_and(r, LANES - 1) > c
    return jnp.where((c >= LANES) | later, 1.0, 0.0).astype(BF16)


def _sb_prompt_kernel(q_ref, k_ref, v_ref, o_ref, qas, qbs, ks, vs, run_ref, acc_ref, *, seq):
    n_full, tail = divmod(seq, LANES)
    lp = ks.shape[0]
    qa, qb = _split_heads(q_ref[0])
    qas[...] = qa
    qbs[...] = qb
    ks[0:seq, :] = k_ref[0].astype(BF16)
    vs[0:seq, :] = v_ref[0].astype(BF16)
    if lp > seq:
        ks[seq:lp, :] = jnp.zeros((lp - seq, LANES), BF16)
        vs[seq:lp, :] = jnp.zeros((lp - seq, LANES), BF16)
    suffix = _suffix_matrix()

    row = lax.broadcasted_iota(jnp.int32, (LANES, LANES), 0)
    col = lax.broadcasted_iota(jnp.int32, (LANES, LANES), 1)
    vis = col < row
    lane = lax.broadcasted_iota(jnp.int32, (1, LANES), 1)

    def super_tile(a, carry):
        r_base = pl.multiple_of(a * SUPER_ROWS, SUPER_ROWS)
        zero = jnp.zeros((LANES, LANES), F32)
        for tq in range(QUERY_GROUP):
            for h in range(2):
                run_ref[tq, h] = zero
                acc_ref[tq, h] = zero

        def group(grp, diagonal):
            base = pl.multiple_of(grp * GROUP_KEYS, GROUP_KEYS)
            vrows = vs[pl.ds(base, GROUP_KEYS), :]
            for h, qs in enumerate((qas, qbs)):
                qsup = qs[pl.ds(r_base, SUPER_ROWS), :]
                log_betas = [[None] * KEY_GROUP for _ in range(QUERY_GROUP)]
                inners = [[None] * KEY_GROUP for _ in range(QUERY_GROUP)]
                totals = [[None] * KEY_GROUP for _ in range(QUERY_GROUP)]
                for g in range(KEY_GROUP):
                    first = g if diagonal else 0
                    z = _dot_nt(qsup[first * LANES:, :], ks[pl.ds(base + g * LANES, LANES), :])
                    log_beta = _log_sigmoid(z)
                    log_keep = log_beta - z
                    if diagonal:
                        top = jnp.where(vis, log_keep[0:LANES], 0.0)
                        log_keep = top if first == QUERY_GROUP - 1 else jnp.concatenate([top, log_keep[LANES:]], axis=0)
                    both = _dot(_hi_lo(log_keep), suffix)
                    for tq in range(first, QUERY_GROUP):
                        s0 = (tq - first) * LANES
                        log_betas[tq][g] = log_beta[s0:s0 + LANES]
                        inners[tq][g] = both[s0:s0 + LANES, 0:LANES]
                        totals[tq][g] = both[s0:s0 + LANES, LANES:]
                weights = []
                for tq in range(QUERY_GROUP):
                    run = run_ref[tq, h]
                    last = tq if diagonal else KEY_GROUP - 1
                    ws = [None] * (last + 1)
                    for g in reversed(range(last + 1)):
                        w = jnp.exp(log_betas[tq][g] + inners[tq][g] + run)
                        if diagonal and g == tq:
                            w = jnp.where(vis, w, 0.0)
                        ws[g] = w.astype(BF16)
                        run = run + totals[tq][g]
                    run_ref[tq, h] = run
                    wrow = ws[0] if last == 0 else jnp.concatenate(ws, axis=1)
                    if diagonal:
                        acc_ref[tq, h] += _dot(wrow, vrows[0:(last + 1) * LANES, :])
                    else:
                        weights.append(wrow)
                if not diagonal:
                    pv = _dot(jnp.concatenate(weights, axis=0), vrows)
                    for tq in range(QUERY_GROUP):
                        acc_ref[tq, h] += pv[tq * LANES:(tq + 1) * LANES]

        group(a, True)

        def lower(t, c):
            group(a - 1 - t, False)
            return c

        lax.fori_loop(0, a, lower, 0)
        for tq in range(QUERY_GROUP):
            o = jnp.where(lane < SB_HEAD_DIM, acc_ref[tq, 0], acc_ref[tq, 1])
            o_ref[0, pl.ds(r_base + tq * LANES, LANES), :] = o.astype(o_ref.dtype)
        return carry

    def full_tile(i, carry):
        r0 = pl.multiple_of(i * LANES, LANES)
        o = _sb_tile(i, LANES, (qas[pl.ds(r0, LANES), :], qbs[pl.ds(r0, LANES), :]), ks, vs, suffix)
        o_ref[0, pl.ds(r0, LANES), :] = o.astype(o_ref.dtype)
        return carry

    n_super = n_full // QUERY_GROUP
    lax.fori_loop(0, n_super, super_tile, 0)
    lax.fori_loop(n_super * QUERY_GROUP, n_full, full_tile, 0)
    if tail:
        r0 = n_full * LANES
        o = _sb_rows_static(n_full, tail, (qas[r0:r0 + tail, :], qbs[r0:r0 + tail, :]), ks, vs, suffix)
        o_ref[0, r0:r0 + tail, :] = o.astype(o_ref.dtype)


def _sb_prompt(q, k, v, layer):
    b, seq, _ = q.shape
    assert seq % BF16_ROWS == 0
    lp = pl.cdiv(pl.cdiv(seq, LANES), KEY_GROUP) * GROUP_KEYS
    blk = lambda: pl.BlockSpec((1, seq, LANES), lambda i, p: (i, 0, p))
    stk = lambda: pl.BlockSpec((None, 1, seq, LANES), lambda i, p: (layer, i, 0, p))
    return pl.pallas_call(
        functools.partial(_sb_prompt_kernel, seq=seq),
        grid=(b, SB_HEADS // 2),
        in_specs=[blk(), stk(), stk()],
        out_specs=blk(),
        out_shape=jax.ShapeDtypeStruct((b, seq, SB_HEADS * SB_HEAD_DIM), BF16),
        scratch_shapes=[pltpu.VMEM((seq, LANES), BF16), pltpu.VMEM((seq, LANES), BF16),
                        pltpu.VMEM((lp, LANES), BF16), pltpu.VMEM((lp, LANES), BF16),
                        pltpu.VMEM((QUERY_GROUP, 2, LANES, LANES), F32),
                        pltpu.VMEM((QUERY_GROUP, 2, LANES, LANES), F32)],
        compiler_params=_params(("arbitrary", "arbitrary")),
        name="stick_prompt",
    )(q, k, v)


def _sb_sample_kernel(q_ref, k_ref, v_ref, ck_ref, cv_ref, o_ref, ks, vs, *, past, n_new):
    lw = ks.shape[0]
    tot = past + n_new
    ks[0:past, :] = ck_ref[0, 0].astype(BF16)
    vs[0:past, :] = cv_ref[0, 0].astype(BF16)
    ks[past:tot, :] = k_ref[0].astype(BF16)
    vs[past:tot, :] = v_ref[0].astype(BF16)
    if lw > tot:
        ks[tot:lw, :] = jnp.zeros((lw - tot, LANES), BF16)
        vs[tot:lw, :] = jnp.zeros((lw - tot, LANES), BF16)
    o = _sb_rows_static(past // LANES, n_new, _split_heads(q_ref[0]), ks, vs, _suffix_matrix())
    o_ref[0] = o.astype(o_ref.dtype)


def _sb_sample(q, k, v, cache_k, cache_v, layer):
    b, n_new, _ = q.shape
    past = cache_k.shape[2]
    assert past % LANES == 0 and n_new <= LANES and n_new % BF16_ROWS == 0
    lw = (past // GROUP_KEYS + 1) * GROUP_KEYS
    new = lambda: pl.BlockSpec((1, n_new, LANES), lambda i, p: (i, 0, p))
    stk = lambda: pl.BlockSpec((None, 1, n_new, LANES), lambda i, p: (layer, i, 0, p))
    old = lambda: pl.BlockSpec((1, 1, past, LANES), lambda i, p: (layer, i, 0, p))
    return pl.pallas_call(
        functools.partial(_sb_sample_kernel, past=past, n_new=n_new),
        grid=(b, SB_HEADS // 2),
        in_specs=[new(), stk(), stk(), old(), old()],
        out_specs=new(),
        out_shape=jax.ShapeDtypeStruct((b, n_new, SB_HEADS * SB_HEAD_DIM), BF16),
        scratch_shapes=[pltpu.VMEM((lw, LANES), BF16), pltpu.VMEM((lw, LANES), BF16)],
        compiler_params=_params(("arbitrary", "arbitrary")),
        name="stick_sample",
    )(q, k, v, cache_k, cache_v)


def _ssd_kernel(xbc_ref, z_ref, dt_ref, hist_ref, h0_ref, cw_ref, cb_ref, dtb_ref, alog_ref,
                dexp_ref, gn_ref, y_ref, hfin_ref, prev8, ht, *, seq):
    d_in = SSM_HEADS * SSM_HEAD_DIM
    gw = d_in // SSM_GROUPS
    n_full, tail = divmod(seq, SSD_BLOCK)
    blk = SSD_BLOCK

    r = lax.broadcasted_iota(jnp.int32, (blk, blk), 0)
    c = lax.broadcasted_iota(jnp.int32, (blk, blk), 1)
    tril = r >= c
    tri = jnp.where(tril, 1.0, 0.0).astype(BF16)
    er = lax.broadcasted_iota(jnp.int32, (LANES, d_in), 0)
    ec = lax.broadcasted_iota(jnp.int32, (LANES, d_in), 1)
    expand = jnp.where(jnp.right_shift(ec, 6) == er, 1.0, 0.0).astype(BF16)
    lane = lax.broadcasted_iota(jnp.int32, (1, LANES), 1)

    prev8[...] = hist_ref[0]
    for g in range(SSM_GROUPS):
        for j in range(gw // LANES):
            rows = h0_ref[0, g * gw + j * LANES:g * gw + (j + 1) * LANES, :]
            ht[g, :, j * LANES:(j + 1) * LANES] = rows.T

    a_neg = -jnp.exp(alog_ref[...])
    cw = cw_ref[...]

    def block(t0, rows):
        xin = xbc_ref[0, pl.ds(t0, rows), :]
        win = jnp.concatenate([prev8[...], xin], axis=0)
        conv = cw[3:4] * win
        for k in range(1, SSM_CONV):
            conv = conv + cw[3 - k:4 - k] * pltpu.roll(win, k, axis=0)
        xbc = _silu(conv[SUBLANES:] + cb_ref[...])
        prev8[...] = xin[rows - SUBLANES:rows]
        dt = _softplus(dt_ref[0, pl.ds(t0, rows), :] + dtb_ref[...])
        z = z_ref[0, pl.ds(t0, rows), :]
        if rows < blk:
            pad = blk - rows
            xbc = jnp.concatenate([xbc, jnp.zeros((pad, xbc.shape[1]), F32)], axis=0)
            dt = jnp.concatenate([dt, jnp.zeros((pad, LANES), F32)], axis=0)
            z = jnp.concatenate([z, jnp.zeros((pad, d_in), F32)], axis=0)
        x = xbc[:, 0:d_in]
        cs = _dot_exact_lhs(tri, dt * a_neg)
        cs_t = cs.T
        dt_e = _dot_exact_rhs(dt, expand)
        cs_e = _dot_exact_rhs(cs, expand)
        xdt = x * dt_e
        xdt_b = xdt.astype(BF16)
        grow = jnp.exp(cs_e)
        cs_last = cs_e[blk - 1:blk, :]
        to_end = (jnp.exp(cs_last - cs_e) * xdt).astype(BF16)
        carry_decay = jnp.exp(cs_last)
        ys = []
        for g in range(SSM_GROUPS):
            bmat = xbc[:, d_in + g * SSM_D_STATE:d_in + (g + 1) * SSM_D_STATE]
            cmat = xbc[:, d_in + (SSM_GROUPS + g) * SSM_D_STATE:d_in + (SSM_GROUPS + g + 1) * SSM_D_STATE]
            cmat_b = cmat.astype(BF16)
            cb = _dot_nt(cmat_b, bmat.astype(BF16))
            hprev = ht[g]
            y_g = _dot(cmat_b, hprev.astype(BF16)) * grow[:, g * gw:(g + 1) * gw]
            parts = []
            for pr in range(gw // LANES):
                l0 = g * gw + pr * LANES
                slab = xdt_b[:, l0:l0 + LANES]
                hd = l0 // SSM_HEAD_DIM
                res = []
                for hh in (hd, hd + 1):
                    decay = jnp.exp(jnp.where(tril, cs[:, hh:hh + 1] - cs_t[hh:hh + 1, :], -jnp.inf))
                    res.append(_dot((cb * decay).astype(BF16), slab))
                parts.append(jnp.where(lane < SSM_HEAD_DIM, res[0], res[1]))
            y_g = y_g + jnp.concatenate(parts, axis=1)
            ht[g] = carry_decay[:, g * gw:(g + 1) * gw] * hprev + _dot(bmat.T.astype(BF16), to_end[:, g * gw:(g + 1) * gw])
            ys.append(y_g)
        y = jnp.concatenate(ys, axis=1)
        y = (y + dexp_ref[...] * x) * _silu(z)
        outs = []
        for g in range(SSM_GROUPS):
            outs.append(_rms(y[:, g * gw:(g + 1) * gw], gn_ref[:, g * gw:(g + 1) * gw]))
        y = jnp.concatenate(outs, axis=1)
        y_ref[0, pl.ds(t0, rows), :] = y[0:rows].astype(y_ref.dtype)

    def full_block(i, carry):
        block(pl.multiple_of(i * blk, blk), blk)
        return carry

    lax.fori_loop(0, n_full, full_block, 0)
    if tail:
        block(n_full * blk, tail)
    for g in range(SSM_GROUPS):
        for j in range(gw // LANES):
            hfin_ref[0, g * gw + j * LANES:g * gw + (j + 1) * LANES, :] = ht[g, :, j * LANES:(j + 1) * LANES].T


def _ssd(xbc, z, dt, hist8, h0, cw, cb, dtb, alog, dexp, gn):
    b, seq, cdim = xbc.shape
    d_in = SSM_HEADS * SSM_HEAD_DIM
    assert seq % SUBLANES == 0 and (seq % SSD_BLOCK) % BF16_ROWS == 0
    per_b = lambda n: pl.BlockSpec((1, seq, n), lambda i: (i, 0, 0))
    const = lambda shape: pl.BlockSpec(shape, lambda i: (0,) * len(shape))
    return pl.pallas_call(
        functools.partial(_ssd_kernel, seq=seq),
        grid=(b,),
        in_specs=[per_b(cdim), per_b(d_in), per_b(LANES),
                  pl.BlockSpec((1, SUBLANES, cdim), lambda i: (i, 0, 0)),
                  pl.BlockSpec((1, d_in, SSM_D_STATE), lambda i: (i, 0, 0)),
                  const((SSM_CONV, cdim)), const((1, cdim)), const((1, LANES)), const((1, LANES)),
                  const((1, d_in)), const((1, d_in))],
        out_specs=[per_b(d_in), pl.BlockSpec((1, d_in, SSM_D_STATE), lambda i: (i, 0, 0))],
        out_shape=[jax.ShapeDtypeStruct((b, seq, d_in), BF16),
                   jax.ShapeDtypeStruct((b, d_in, SSM_D_STATE), F32)],
        scratch_shapes=[pltpu.VMEM((SUBLANES, cdim), F32),
                        pltpu.VMEM((SSM_GROUPS, SSM_D_STATE, d_in // SSM_GROUPS), F32)],
        compiler_params=_params(("arbitrary",)),
        name="ssd",
    )(xbc, z, dt, hist8, h0, cw, cb, dtb, alog, dexp, gn)


def _out_ffn_kernel(x_ref, da_ref, y_ref, sb_ref, wo_ref, g_ref, wg_ref, wu_ref, wd_ref,
                    fw_ref, fb_ref, init_ref, o_ref, last_ref, act_ref):
    ts = x_ref.shape[1]
    d_ff = wg_ref.shape[1]
    nd = da_ref.shape[2]
    ny = y_ref.shape[2]

    @pl.when(pl.program_id(1) == 0)
    def _():
        last_ref[...] = init_ref[...]

    xr = (x_ref[0] + _dot(da_ref[0], wo_ref[0:nd, :]) + _dot(y_ref[0], wo_ref[nd:nd + ny, :])
          + _dot(sb_ref[0], wo_ref[nd + ny:, :]))
    h2 = _rms(xr, g_ref[...]).astype(BF16)
    row8 = lax.broadcasted_iota(jnp.int32, (SUBLANES, FF_CHUNK), 0)
    for c0 in range(0, d_ff, FF_CHUNK):
        cols = slice(c0, c0 + FF_CHUNK)
        g = _dot(h2, wg_ref[:, cols])
        u = _dot(h2, wu_ref[:, cols])
        hist = last_ref[0, :, cols]
        conv = fw_ref[2:3, cols] * g + fb_ref[:, cols]
        for k in (1, 2):
            shifted = pltpu.roll(g, k, axis=0)
            top = jnp.where(row8 < k, pltpu.roll(hist, k, axis=0), shifted[0:SUBLANES])
            shifted = jnp.concatenate([top, shifted[SUBLANES:]], axis=0)
            conv = conv + fw_ref[2 - k:3 - k, cols] * shifted
        last_ref[0, :, cols] = g[ts - SUBLANES:ts]
        act_ref[:, cols] = (_silu(conv) * u).astype(BF16)
    o_ref[0] = xr + _dot(act_ref[...], wd_ref[...])


def _out_ffn(x, da_o, y, sb_o, wo, g, wg, wu, wd, fw, fb, init8):
    b, seq, d = x.shape
    d_ff = wg.shape[1]
    assert d_ff % FF_CHUNK == 0
    ts = _pick_tile(seq, 704, BF16_ROWS)
    row = lambda n: pl.BlockSpec((1, ts, n), lambda i, j: (i, j, 0))
    state = lambda: pl.BlockSpec((1, SUBLANES, d_ff), lambda i, j: (i, 0, 0))
    return pl.pallas_call(
        _out_ffn_kernel,
        grid=(b, seq // ts),
        in_specs=[row(d), row(da_o.shape[2]), row(y.shape[2]), row(sb_o.shape[2]),
                  _resident(wo.shape), _resident((1, d)), _resident(wg.shape), _resident(wu.shape),
                  _resident(wd.shape), _resident(fw.shape), _resident((1, d_ff)), state()],
        out_specs=[row(d), state()],
        out_shape=[jax.ShapeDtypeStruct((b, seq, d), F32),
                   jax.ShapeDtypeStruct((b, SUBLANES, d_ff), F32)],
        scratch_shapes=[pltpu.VMEM((ts, d_ff), BF16)],
        compiler_params=_params(("arbitrary", "arbitrary")),
        name="out_ffn",
    )(x, da_o, y, sb_o, wo, g, wg, wu, wd, fw, fb, init8)


def _final_kernel(x_ref, g_ref, o_ref, *, skip):
    n = o_ref.shape[1]
    o_ref[0] = _rms(x_ref[0, skip:skip + n, :], g_ref[...])


def _final_norm(x, g, skip):
    b, seq, d = x.shape
    n = seq - skip
    assert skip % SUBLANES == 0
    return pl.pallas_call(
        functools.partial(_final_kernel, skip=skip),
        grid=(b,),
        in_specs=[pl.BlockSpec((1, seq, d), lambda i: (i, 0, 0)), pl.BlockSpec((1, d), lambda i: (0, 0))],
        out_specs=pl.BlockSpec((1, n, d), lambda i: (i, 0, 0)),
        out_shape=jax.ShapeDtypeStruct((b, n, d), F32),
        compiler_params=_params(("arbitrary",)),
        name="final_norm",
    )(x, g)


def _pad_rows_top(a, rows):
    return jnp.pad(a, ((0, 0), (rows - a.shape[1], 0), (0, 0)))


def _layer(x, past, layer, depth, stacks, p, lam_init, bias_prompt, bias_sample):
    b, seq, d = x.shape
    proj = _in_proj(x.reshape(b * seq, d), p["norm_mix"], p["w_in"], layer, depth, stacks)
    da_q, da_k, da_v, ssm_z, ssm_xbc, sb_q, sb_k, sb_v, ssm_dt = proj
    stacks = (da_k, da_v, sb_k, sb_v)
    da_q, ssm_z, ssm_xbc, sb_q, ssm_dt = [a.reshape(b, seq, -1) for a in (da_q, ssm_z, ssm_xbc, sb_q, ssm_dt)]
    da_k, da_v, sb_k, sb_v = [a.reshape(depth, b, seq, -1) for a in stacks]
    if past is None:
        da_o = _da_prompt(da_q, da_k, da_v, layer, bias_prompt, p["lamp"], p["da_subln"], lam_init)
        sb_o = _sb_prompt(sb_q, sb_k, sb_v, layer)
        hist8 = jnp.zeros((b, SUBLANES, ssm_xbc.shape[2]), F32)
        h0 = jnp.zeros((b, SSM_HEADS * SSM_HEAD_DIM, SSM_D_STATE), F32)
        ffn_init = jnp.zeros((b, SUBLANES, p["w_gate"].shape[1]), F32)
    else:
        c_da_k, c_da_v, c_sb_k, c_sb_v, state_ssm, state_conv, state_ffn = past
        da_o = _da_sample(da_q, da_k, da_v, c_da_k, c_da_v, layer, bias_sample, p["lamp"], p["da_subln"], lam_init)
        sb_o = _sb_sample(sb_q, sb_k, sb_v, c_sb_k, c_sb_v, layer)
        hist8 = _pad_rows_top(state_conv[layer], SUBLANES)
        h0 = state_ssm[layer].reshape(b, SSM_HEADS * SSM_HEAD_DIM, SSM_D_STATE)
        ffn_init = _pad_rows_top(state_ffn[layer], SUBLANES)
    y, h_final = _ssd(ssm_xbc, ssm_z, ssm_dt, hist8, h0, p["conv_w"], p["conv_b"], p["dt_bias"],
                      p["a_log"], p["d_exp"], p["ssm_norm"])
    x_new, ffn_last = _out_ffn(x, da_o, y, sb_o, p["w_out"], p["norm_ffn"], p["w_gate"], p["w_up"],
                               p["w_down"], p["fconv_w"], p["fconv_b"], ffn_init)
    conv_in_tail = jnp.concatenate([hist8, ssm_xbc[:, -min(seq, SUBLANES):]], axis=1)
    states = (h_final.reshape(b, SSM_HEADS, SSM_HEAD_DIM, SSM_D_STATE),
              conv_in_tail[:, -(SSM_CONV - 1):],
              ffn_last[:, -(FFN_CONV - 1):])
    return x_new, stacks, states


def _pad_lanes(v, n):
    return jnp.pad(v, (0, n - v.shape[0])).reshape(1, n)


def kernel(x_prompt, x_sample, cache_da_k, cache_da_v, cache_sb_k, cache_sb_v, state_ssm, state_ssm_conv, state_ffn_conv, meta_tokens, rel_bias_table, w_in, w_out, norm_mix, norm_ffn, da_lambda_q1, da_lambda_k1, da_lambda_q2, da_lambda_k2, da_subln, ssm_conv_w, ssm_conv_b, ssm_dt_bias, ssm_a_log, ssm_d, ssm_norm, ffn_w_gate, ffn_w_up, ffn_w_down, ffn_conv_w, ffn_conv_b, final_norm):
    depth = w_in.shape[0]
    b = x_prompt.shape[0]
    d = x_prompt.shape[2]
    sb, n_new, _ = x_sample.shape
    past_len = cache_da_k.shape[2]
    n_meta = meta_tokens.shape[0]
    assert n_meta == N_META

    xp = jnp.concatenate([jnp.broadcast_to(meta_tokens[None], (b, n_meta, d)), x_prompt], axis=1)
    xs = x_sample
    bias_prompt, bias_sample = _bias_tiles(rel_bias_table, past_len, n_new)
    cache = (cache_da_k.reshape(depth, sb, past_len, -1), cache_da_v.reshape(depth, sb, past_len, -1),
             cache_sb_k.reshape(depth, sb, past_len, -1), cache_sb_v.reshape(depth, sb, past_len, -1),
             state_ssm, state_ssm_conv, state_ffn_conv)

    p_states, s_states = [], []
    p_stacks = s_stacks = None
    for i in range(depth):
        wi = w_in[i]
        w_r = jnp.concatenate([wi[:, :_OFF_DT], wi[:, _OFF_DT + SSM_HEADS:],
                               jnp.pad(wi[:, _OFF_DT:_OFF_DT + SSM_HEADS], ((0, 0), (0, LANES - SSM_HEADS)))],
                              axis=1).astype(BF16)
        p = dict(
            w_in=w_r, w_out=w_out[i].astype(BF16),
            norm_mix=norm_mix[i].reshape(1, d), norm_ffn=norm_ffn[i].reshape(1, d),
            lamp=jnp.stack([da_lambda_q1[i], da_lambda_k1[i], da_lambda_q2[i], da_lambda_k2[i]]),
            da_subln=da_subln[i].reshape(1, DA_VDIM),
            conv_w=ssm_conv_w[i], conv_b=ssm_conv_b[i].reshape(1, -1),
            dt_bias=_pad_lanes(ssm_dt_bias[i], LANES), a_log=_pad_lanes(ssm_a_log[i], LANES),
            d_exp=jnp.repeat(ssm_d[i], SSM_HEAD_DIM).reshape(1, -1), ssm_norm=ssm_norm[i].reshape(1, -1),
            w_gate=ffn_w_gate[i].astype(BF16), w_up=ffn_w_up[i].astype(BF16), w_down=ffn_w_down[i].astype(BF16),
            fconv_w=ffn_conv_w[i], fconv_b=ffn_conv_b[i].reshape(1, -1),
        )
        lam_init = 0.8 - 0.6 * math.exp(-0.3 * i)
        xp, p_stacks, st_p = _layer(xp, None, i, depth, p_stacks, p, lam_init, bias_prompt, bias_sample)
        xs, s_stacks, st_s = _layer(xs, cache, i, depth, s_stacks, p, lam_init, bias_prompt, bias_sample)
        p_states.append(st_p)
        s_states.append(st_s)

    g_fin = final_norm.reshape(1, d)
    y_prompt = _final_norm(xp, g_fin, n_meta)
    y_sample = _final_norm(xs, g_fin, 0)

    def outputs(stacks, states, nb, seq):
        da_k, da_v, sb_k, sb_v = stacks
        return (da_k.reshape(depth, nb, seq, DA_HEADS, 2, DA_HEAD_DIM), da_v.reshape(depth, nb, seq, DA_HEADS, DA_VDIM),
                sb_k.reshape(depth, nb, seq, SB_HEADS, SB_HEAD_DIM), sb_v.reshape(depth, nb, seq, SB_HEADS, SB_HEAD_DIM)
                ) + tuple(jnp.stack([s[j] for s in states], axis=0) for j in range(3))

    return ((y_prompt, y_sample) + outputs(p_stacks, p_states, b, n_meta + x_prompt.shape[1])
            + outputs(s_stacks, s_states, sb, n_new))
```

```python
import functools
import math

import jax
import jax.numpy as jnp
from jax import lax
from jax.experimental import pallas as pl
from jax.experimental.pallas import tpu as pltpu

F32 = jnp.float32
BF16 = jnp.bfloat16

LANES = 128
SUBLANES = 8
BF16_ROWS = 16
V7X_VMEM_BYTES = 64 * 1024 * 1024
VMEM_LIMIT = 56 * 1024 * 1024

CHUNK = 64
N_META = 16
EPS = 1e-6
NEG_INF = -1e30
DA_HEADS = 4
DA_HEAD_DIM = 64
DA_VDIM = 128
SB_HEADS = 8
SB_HEAD_DIM = 64
SSM_HEADS = 8
SSM_HEAD_DIM = 64
SSM_GROUPS = 2
SSM_D_STATE = 128
SSM_CONV = 4
FFN_CONV = 3
REL_BUCKETS = 32
REL_MAX_DIST = 128
QK_SCALE = 0.125
SSD_BLOCK = 128
FF_CHUNK = 256
KEY_GROUP = 4
GROUP_KEYS = KEY_GROUP * LANES
QUERY_GROUP = KEY_GROUP
SUPER_ROWS = QUERY_GROUP * LANES
BIAS_HIDDEN = 4


def _dot(a, b):
    return jnp.dot(a, b, preferred_element_type=F32)


def _dot_nt(a, b):
    return lax.dot_general(a, b, (((1,), (1,)), ((), ())), preferred_element_type=F32)


def _split3(x):
    hi = x.astype(BF16)
    r1 = x - hi.astype(F32)
    mid = r1.astype(BF16)
    lo = (r1 - mid.astype(F32)).astype(BF16)
    return hi, mid, lo


def _dot_exact_rhs(x, m_bf16):
    hi, mid, lo = _split3(x)
    return _dot(hi, m_bf16) + _dot(mid, m_bf16) + _dot(lo, m_bf16)


def _dot_exact_lhs(m_bf16, x):
    hi, mid, lo = _split3(x)
    return _dot(m_bf16, hi) + _dot(m_bf16, mid) + _dot(m_bf16, lo)


def _rms(x, g):
    return x * lax.rsqrt(jnp.mean(x * x, axis=-1, keepdims=True) + EPS) * g


def _silu(x):
    return x * (1.0 / (1.0 + jnp.exp(-x)))


def _softplus(x):
    return jnp.maximum(x, 0.0) + jnp.log1p(jnp.exp(-jnp.abs(x)))


def _pick_tile(n, cap, mult):
    best = None
    for t in range(mult, min(n, cap) + 1, mult):
        if n % t == 0:
            best = t
    assert best is not None, (n, cap, mult)
    return best


def _resident(shape):
    nd = len(shape)
    return pl.BlockSpec(shape, lambda *_: (0,) * nd, pipeline_mode=pl.Buffered(1))


def _params(sem):
    return pltpu.CompilerParams(dimension_semantics=sem, vmem_limit_bytes=VMEM_LIMIT)


def _rel_bucket(rel):
    half = REL_BUCKETS // 2
    max_exact = half // 2
    n = jnp.abs(rel)
    nf = jnp.maximum(n, 1).astype(jnp.float32)
    large = max_exact + (jnp.log(nf / max_exact) / math.log(REL_MAX_DIST / max_exact)
                         * (half - max_exact)).astype(jnp.int32)
    large = jnp.minimum(large, half - 1)
    return jnp.where(rel > 0, half, 0) + jnp.where(n < max_exact, n, large)


def _bias_kernel(tab_ref, bp_ref, bs_ref, op_ref, os_ref, *, past, n_new):
    h = pl.program_id(0)

    def lookup(bucket):
        val = jnp.zeros(bucket.shape, F32)
        for b in range(REL_BUCKETS):
            val = jnp.where(bucket == b, tab_ref[b, h], val)
        return val

    r = lax.broadcasted_iota(jnp.int32, (LANES, LANES), 0)
    c = lax.broadcasted_iota(jnp.int32, (LANES, LANES), 1)
    qc = jnp.right_shift(r - N_META, 6)
    kc = jnp.right_shift(c - N_META, 6)
    for slot in range(BIAS_HIDDEN):
        val = lookup(bp_ref[slot])
        if slot > 0:
            d = slot - 2
            val = jnp.where(2 * d + kc - qc <= 0, val, NEG_INF)
        op_ref[0, slot] = val
    op_ref[0, BIAS_HIDDEN] = jnp.full((LANES, LANES), NEG_INF, F32)

    lw = bs_ref.shape[1]
    rs = lax.broadcasted_iota(jnp.int32, (n_new, lw), 0) + past
    cs = lax.broadcasted_iota(jnp.int32, (n_new, lw), 1)
    vis = (jnp.right_shift(cs, 6) <= jnp.right_shift(rs, 6)) & (cs < past + n_new)
    os_ref[0] = jnp.where(vis, lookup(bs_ref[...]), NEG_INF)


def _bias_tiles(rel_table, past, n_new):
    lw = pl.cdiv(past + n_new, LANES) * LANES
    r = jnp.arange(LANES, dtype=jnp.int32)[:, None]
    c = jnp.arange(LANES, dtype=jnp.int32)[None, :]
    far = jnp.full((LANES, LANES), -2 * LANES, jnp.int32)
    bp = jnp.stack([_rel_bucket(far)] + [_rel_bucket(d * LANES + c - r) for d in (-1, 0, 1)])
    qs = past + jnp.arange(n_new, dtype=jnp.int32)[:, None]
    ks = jnp.arange(lw, dtype=jnp.int32)[None, :]
    bs = _rel_bucket(ks - qs)
    return pl.pallas_call(
        functools.partial(_bias_kernel, past=past, n_new=n_new),
        grid=(DA_HEADS,),
        in_specs=[pl.BlockSpec(memory_space=pltpu.SMEM),
                  pl.BlockSpec((4, LANES, LANES), lambda h: (0, 0, 0)),
                  pl.BlockSpec((n_new, lw), lambda h: (0, 0))],
        out_specs=[pl.BlockSpec((1, BIAS_HIDDEN + 1, LANES, LANES), lambda h: (h, 0, 0, 0)),
                   pl.BlockSpec((1, n_new, lw), lambda h: (h, 0, 0))],
        out_shape=[jax.ShapeDtypeStruct((DA_HEADS, BIAS_HIDDEN + 1, LANES, LANES), F32),
                   jax.ShapeDtypeStruct((DA_HEADS, n_new, lw), F32)],
        name="rel_bias_tiles",
    )(rel_table, bp, bs)


_SEG = (("da_q", 512, BF16), ("da_k", 512, F32), ("da_v", 512, F32), ("ssm_z", 512, F32),
        ("ssm_xbc", 1024, F32), ("sb_q", 512, BF16), ("sb_k", 512, F32), ("sb_v", 512, F32),
        ("ssm_dt", LANES, F32))
_N_PROJ = sum(s[1] for s in _SEG)
_OFF_DT = 3072


def _inproj_kernel(x_ref, g_ref, w_ref, *out_refs):
    h = _rms(x_ref[...], g_ref[...]).astype(BF16)
    off = 0
    for (_, n, dt), o_ref in zip(_SEG, out_refs):
        o_ref[...] = _dot(h, w_ref[:, off:off + n]).astype(dt)
        off += n


def _in_proj(x2d, g, w_r):
    t, d = x2d.shape
    tm = _pick_tile(t, 512, BF16_ROWS)
    return pl.pallas_call(
        _inproj_kernel,
        grid=(t // tm,),
        in_specs=[pl.BlockSpec((tm, d), lambda i: (i, 0)), _resident((1, d)), _resident((d, _N_PROJ))],
        out_specs=[pl.BlockSpec((tm, n), lambda i: (i, 0)) for _, n, _ in _SEG],
        out_shape=[jax.ShapeDtypeStruct((t, n), dt) for _, n, dt in _SEG],
        compiler_params=_params(("arbitrary",)),
        name="in_proj",
    )(x2d, g, w_r)


def _lambda(lamp_ref, lam_init):
    lp = lamp_ref[...]
    return (jnp.exp(jnp.sum(lp[0:1] * lp[1:2], axis=1, keepdims=True))
            - jnp.exp(jnp.sum(lp[2:3] * lp[3:4], axis=1, keepdims=True)) + lam_init)


def _split_maps(q_bf16):
    lane = lax.broadcasted_iota(jnp.int32, (1, LANES), 1)
    qf = q_bf16.astype(F32) * QK_SCALE
    return (jnp.where(lane < DA_HEAD_DIM, qf, 0.0).astype(BF16),
            jnp.where(lane >= DA_HEAD_DIM, qf, 0.0).astype(BF16))


def _da_prompt_kernel(q_ref, k_ref, v_ref, bm_ref, lamp_ref, subln_ref, o_ref,
                      q1s, q2s, ks, vs, sbuf, sbig, mx_ref, l_ref, acc_ref, *, seq, lam_init):
    n_full, tail = divmod(seq, LANES)
    n_blk = n_full + (1 if tail else 0)
    lp = ks.shape[0]
    q1, q2 = _split_maps(q_ref[0])
    q1s[...] = q1
    q2s[...] = q2
    ks[0:seq, :] = k_ref[0].astype(BF16)
    vs[0:seq, :] = v_ref[0].astype(BF16)
    ks[seq:lp, :] = jnp.zeros((lp - seq, LANES), BF16)
    vs[seq:lp, :] = jnp.zeros((lp - seq, LANES), BF16)
    lam = _lambda(lamp_ref, lam_init)
    gsub = subln_ref[...] * (1.0 - lam_init)

    def tile(i, rows, r0):
        q1t = q1s[pl.ds(r0, rows), :]
        q2t = q2s[pl.ds(r0, rows), :]
        n_grp = (jnp.minimum(i + 2, n_blk) + KEY_GROUP - 1) // KEY_GROUP

        def scores(grp, carry):
            m1, m2 = carry
            base = pl.multiple_of(grp * GROUP_KEYS, GROUP_KEYS)
            for g in range(KEY_GROUP):
                kb = grp * KEY_GROUP + g
                kblk = ks[pl.ds(base + g * LANES, LANES), :]
                slot = jnp.where(kb >= n_blk, BIAS_HIDDEN, jnp.clip(kb - i + 2, 0, BIAS_HIDDEN))
                bias = bm_ref[0, slot, 0:rows, :]
                s1 = _dot_nt(q1t, kblk) + bias
                s2 = _dot_nt(q2t, kblk) + bias
                sbuf[0, grp, 0:rows, g * LANES:(g + 1) * LANES] = s1
                sbuf[1, grp, 0:rows, g * LANES:(g + 1) * LANES] = s2
                m1 = jnp.maximum(m1, s1)
                m2 = jnp.maximum(m2, s2)
            return m1, m2

        ninf = jnp.full((rows, LANES), -jnp.inf, F32)
        m1, m2 = lax.fori_loop(0, n_grp, scores, (ninf, ninf))
        m1 = jnp.max(m1, axis=1, keepdims=True)
        m2 = jnp.max(m2, axis=1, keepdims=True)

        def exp_pv(grp, carry):
            l1, l2, a1, a2 = carry
            base = pl.multiple_of(grp * GROUP_KEYS, GROUP_KEYS)
            ps1, ps2 = [], []
            for g in range(KEY_GROUP):
                p1 = jnp.exp(sbuf[0, grp, 0:rows, g * LANES:(g + 1) * LANES] - m1)
                p2 = jnp.exp(sbuf[1, grp, 0:rows, g * LANES:(g + 1) * LANES] - m2)
                l1 = l1 + p1
                l2 = l2 + p2
                ps1.append(p1.astype(BF16))
                ps2.append(p2.astype(BF16))
            vrows = vs[pl.ds(base, GROUP_KEYS), :]
            a1 = a1 + _dot(jnp.concatenate(ps1, axis=1), vrows)
            a2 = a2 + _dot(jnp.concatenate(ps2, axis=1), vrows)
            return l1, l2, a1, a2

        zero = jnp.zeros((rows, LANES), F32)
        l1, l2, a1, a2 = lax.fori_loop(0, n_grp, exp_pv, (zero, zero, zero, zero))
        c1 = 1.0 / jnp.sum(l1, axis=1, keepdims=True)
        c2 = lam / jnp.sum(l2, axis=1, keepdims=True)
        o_ref[0, pl.ds(r0, rows), :] = _rms(a1 * c1 - a2 * c2, gsub).astype(o_ref.dtype)

    def super_tile(a, carry):
        r_base = pl.multiple_of(a * SUPER_ROWS, SUPER_ROWS)
        k_base = pl.multiple_of(a * GROUP_KEYS, GROUP_KEYS)
        maps = (q1s, q2s)
        ninf = jnp.full((LANES, LANES), -jnp.inf, F32)
        zero = jnp.zeros((LANES, LANES), F32)
        for m in range(2):
            for tq in range(QUERY_GROUP):
                mx_ref[m, tq] = ninf
                l_ref[m, tq] = zero
                acc_ref[m, tq] = zero

        def scores_lower(grp, c):
            base = pl.multiple_of(grp * GROUP_KEYS, GROUP_KEYS)
            for m in range(2):
                qsup = maps[m][pl.ds(r_base, SUPER_ROWS), :]
                mx = [mx_ref[m, tq] for tq in range(QUERY_GROUP)]
                for g in range(KEY_GROUP):
                    kb = grp * KEY_GROUP + g
                    s_all = _dot_nt(qsup, ks[pl.ds(base + g * LANES, LANES), :])
                    for tq in range(QUERY_GROUP):
                        slot = jnp.clip(kb - (a * QUERY_GROUP + tq) + 2, 0, BIAS_HIDDEN)
                        s = s_all[tq * LANES:(tq + 1) * LANES] + bm_ref[0, slot]
                        sbig[m, tq, kb] = s
                        mx[tq] = jnp.maximum(mx[tq], s)
                for tq in range(QUERY_GROUP):
                    mx_ref[m, tq] = mx[tq]
            return c

        lax.fori_loop(0, a, scores_lower, 0)

        for m in range(2):
            qsup = maps[m][pl.ds(r_base, SUPER_ROWS), :]
            mx = [mx_ref[m, tq] for tq in range(QUERY_GROUP)]
            for g in range(KEY_GROUP + 1):
                first = max(g - 1, 0)
                kb = a * KEY_GROUP + g
                s_all = _dot_nt(qsup[first * LANES:, :], ks[pl.ds(k_base + g * LANES, LANES), :])
                for tq in range(first, QUERY_GROUP):
                    slot = 0 if g - tq <= -2 else g - tq + 2
                    if g == KEY_GROUP:
                        slot = jnp.where(kb >= n_blk, BIAS_HIDDEN, slot)
                    s = s_all[(tq - first) * LANES:(tq - first + 1) * LANES] + bm_ref[0, slot]
                    sbig[m, tq, kb] = s
                    mx[tq] = jnp.maximum(mx[tq], s)
            for tq in range(QUERY_GROUP):
                mx_ref[m, tq] = jnp.broadcast_to(jnp.max(mx[tq], axis=1, keepdims=True), (LANES, LANES))

        def pv_lower(grp, c):
            base = pl.multiple_of(grp * GROUP_KEYS, GROUP_KEYS)
            vrows = vs[pl.ds(base, GROUP_KEYS), :]
            for m in range(2):
                prows = []
                for tq in range(QUERY_GROUP):
                    mrow = mx_ref[m, tq]
                    lsum = l_ref[m, tq]
                    ps = []
                    for g in range(KEY_GROUP):
                        p = jnp.exp(sbig[m, tq, grp * KEY_GROUP + g] - mrow)
                        lsum = lsum + p
                        ps.append(p.astype(BF16))
                    l_ref[m, tq] = lsum
                    prows.append(jnp.concatenate(ps, axis=1))
                pv = _dot(jnp.concatenate(prows, axis=0), vrows)
                for tq in range(QUERY_GROUP):
                    acc_ref[m, tq] += pv[tq * LANES:(tq + 1) * LANES]
            return c

        lax.fori_loop(0, a, pv_lower, 0)

        for tq in range(QUERY_GROUP):
            nb = tq + 2
            vrows = vs[pl.ds(k_base, nb * LANES), :]
            outs = []
            for m in range(2):
                mrow = mx_ref[m, tq]
                lsum = l_ref[m, tq]
                ps = []
                for g in range(nb):
                    p = jnp.exp(sbig[m, tq, a * KEY_GROUP + g] - mrow)
                    lsum = lsum + p
                    ps.append(p.astype(BF16))
                acc = acc_ref[m, tq] + _dot(jnp.concatenate(ps, axis=1), vrows)
                outs.append((acc, jnp.sum(lsum, axis=1, keepdims=True)))
            o = outs[0][0] * (1.0 / outs[0][1]) - outs[1][0] * (lam / outs[1][1])
            o_ref[0, pl.ds(r_base + tq * LANES, LANES), :] = _rms(o, gsub).astype(o_ref.dtype)
        return carry

    def full_tile(i, carry):
        tile(i, LANES, pl.multiple_of(i * LANES, LANES))
        return carry

    n_super = n_full // QUERY_GROUP
    lax.fori_loop(0, n_super, super_tile, 0)
    lax.fori_loop(n_super * QUERY_GROUP, n_full, full_tile, 0)
    if tail:
        r0 = n_full * LANES
        nk = n_blk * LANES
        tiles = [bm_ref[0, 0, 0:tail, :]] * max(n_full - 1, 0)
        if n_full >= 1:
            tiles.append(bm_ref[0, 1, 0:tail, :])
        tiles.append(bm_ref[0, 2, 0:tail, :])
        bias = tiles[0] if len(tiles) == 1 else jnp.concatenate(tiles, axis=1)
        kall = ks[0:nk, :]
        s1 = _dot_nt(q1s[r0:r0 + tail, :], kall) + bias
        s2 = _dot_nt(q2s[r0:r0 + tail, :], kall) + bias
        p1 = jnp.exp(s1 - jnp.max(s1, axis=1, keepdims=True))
        p2 = jnp.exp(s2 - jnp.max(s2, axis=1, keepdims=True))
        c1 = 1.0 / jnp.sum(p1, axis=1, keepdims=True)
        c2 = lam / jnp.sum(p2, axis=1, keepdims=True)
        acc = _dot((p1 * c1 - p2 * c2).astype(BF16), vs[0:nk, :])
        o_ref[0, r0:r0 + tail, :] = _rms(acc, gsub).astype(o_ref.dtype)


def _da_prompt(q, k, v, bm, lamp, subln, lam_init):
    b, seq, _ = q.shape
    assert seq % BF16_ROWS == 0 and (seq - N_META) % CHUNK == 0
    n_blk = pl.cdiv(seq, LANES)
    n_grp = pl.cdiv(n_blk + 1, KEY_GROUP)
    lp = n_grp * GROUP_KEYS
    n_big = (seq // SUPER_ROWS) * QUERY_GROUP + 1
    stat = lambda: pltpu.VMEM((2, QUERY_GROUP, LANES, LANES), F32)
    blk = lambda: pl.BlockSpec((1, seq, LANES), lambda i, h: (i, 0, h))
    return pl.pallas_call(
        functools.partial(_da_prompt_kernel, seq=seq, lam_init=lam_init),
        grid=(b, DA_HEADS),
        in_specs=[blk(), blk(), blk(),
                  pl.BlockSpec((1, BIAS_HIDDEN + 1, LANES, LANES), lambda i, h: (h, 0, 0, 0)),
                  pl.BlockSpec((4, DA_HEAD_DIM), lambda i, h: (0, 0)),
                  pl.BlockSpec((1, DA_VDIM), lambda i, h: (0, 0))],
        out_specs=blk(),
        out_shape=jax.ShapeDtypeStruct((b, seq, DA_HEADS * DA_VDIM), BF16),
        scratch_shapes=[pltpu.VMEM((seq, LANES), BF16), pltpu.VMEM((seq, LANES), BF16),
                        pltpu.VMEM((lp, LANES), BF16), pltpu.VMEM((lp, LANES), BF16),
                        pltpu.VMEM((2, n_grp, LANES, GROUP_KEYS), F32),
                        pltpu.VMEM((2, QUERY_GROUP, n_big, LANES, LANES), F32), stat(), stat(), stat()],
        compiler_params=_params(("arbitrary", "arbitrary")),
        name="diff_attn_prompt",
    )(q, k, v, bm, lamp, subln)


def _da_sample_kernel(q_ref, k_ref, v_ref, ck_ref, cv_ref, bs_ref, lamp_ref, subln_ref, o_ref,
                      ks, vs, *, past, n_new, lam_init):
    lw = ks.shape[0]
    tot = past + n_new
    ks[0:past, :] = ck_ref[0, 0].astype(BF16)
    vs[0:past, :] = cv_ref[0, 0].astype(BF16)
    ks[past:tot, :] = k_ref[0].astype(BF16)
    vs[past:tot, :] = v_ref[0].astype(BF16)
    if lw > tot:
        ks[tot:lw, :] = jnp.zeros((lw - tot, LANES), BF16)
        vs[tot:lw, :] = jnp.zeros((lw - tot, LANES), BF16)
    q1, q2 = _split_maps(q_ref[0])
    bias = bs_ref[0]
    kall = ks[...]
    s1 = _dot_nt(q1, kall) + bias
    s2 = _dot_nt(q2, kall) + bias
    p1 = jnp.exp(s1 - jnp.max(s1, axis=1, keepdims=True))
    p2 = jnp.exp(s2 - jnp.max(s2, axis=1, keepdims=True))
    c1 = 1.0 / jnp.sum(p1, axis=1, keepdims=True)
    c2 = _lambda(lamp_ref, lam_init) / jnp.sum(p2, axis=1, keepdims=True)
    acc = _dot((p1 * c1 - p2 * c2).astype(BF16), vs[...])
    o_ref[0] = _rms(acc, subln_ref[...] * (1.0 - lam_init)).astype(o_ref.dtype)


def _da_sample(q, k, v, cache_k, cache_v, layer, bs, lamp, subln, lam_init):
    b, n_new, _ = q.shape
    past = cache_k.shape[2]
    assert past % BF16_ROWS == 0 and n_new % BF16_ROWS == 0
    lw = bs.shape[2]
    new = lambda: pl.BlockSpec((1, n_new, LANES), lambda i, h: (i, 0, h))
    old = lambda: pl.BlockSpec((1, 1, past, LANES), lambda i, h: (layer, i, 0, h))
    return pl.pallas_call(
        functools.partial(_da_sample_kernel, past=past, n_new=n_new, lam_init=lam_init),
        grid=(b, DA_HEADS),
        in_specs=[new(), new(), new(), old(), old(),
                  pl.BlockSpec((1, n_new, lw), lambda i, h: (h, 0, 0)),
                  pl.BlockSpec((4, DA_HEAD_DIM), lambda i, h: (0, 0)),
                  pl.BlockSpec((1, DA_VDIM), lambda i, h: (0, 0))],
        out_specs=new(),
        out_shape=jax.ShapeDtypeStruct((b, n_new, DA_HEADS * DA_VDIM), BF16),
        scratch_shapes=[pltpu.VMEM((lw, LANES), BF16), pltpu.VMEM((lw, LANES), BF16)],
        compiler_params=_params(("arbitrary", "arbitrary")),
        name="diff_attn_sample",
    )(q, k, v, cache_k, cache_v, bs, lamp, subln)


def _split_heads(q_bf16):
    lane = lax.broadcasted_iota(jnp.int32, (1, LANES), 1)
    qf = q_bf16.astype(F32) * QK_SCALE
    return (jnp.where(lane < SB_HEAD_DIM, qf, 0.0).astype(BF16),
            jnp.where(lane >= SB_HEAD_DIM, qf, 0.0).astype(BF16))


def _log_sigmoid(z):
    neg_abs = lax.bitcast_convert_type(lax.bitcast_convert_type(z, jnp.uint32) | jnp.uint32(0x80000000), F32)
    return jnp.minimum(z, 0.0) - jnp.log(1.0 + jnp.exp(neg_abs))


def _hi_lo(x):
    hi = lax.bitcast_convert_type(lax.bitcast_convert_type(x, jnp.uint32) & jnp.uint32(0xFFFF0000), F32)
    return jnp.concatenate([hi.astype(BF16), (x - hi).astype(BF16)], axis=1)


def _sb_tile(i, rows, qts, ks, vs, suffix):
    row = lax.broadcasted_iota(jnp.int32, (rows, LANES), 0)
    col = lax.broadcasted_iota(jnp.int32, (rows, LANES), 1)
    col_minus_row = col - row
    top = i // KEY_GROUP

    def group(grp, runs, accs, diagonal):
        base = pl.multiple_of(grp * GROUP_KEYS, GROUP_KEYS)
        vrows = vs[pl.ds(base, GROUP_KEYS), :]
        new_runs, new_accs = [], []
        for qt, run, acc in zip(qts, runs, accs):
            log_betas, inners, totals, visible = [], [], [], []
            for g in range(KEY_GROUP):
                z = _dot_nt(qt, ks[pl.ds(base + g * LANES, LANES), :])
                log_beta = _log_sigmoid(z)
                log_keep = log_beta - z
                if diagonal:
                    vis = col_minus_row < (i - grp * KEY_GROUP - g) * LANES
                    log_keep = jnp.where(vis, log_keep, 0.0)
                    visible.append(vis)
                inner, total = _suffix_sums(log_keep, suffix)
                log_betas.append(log_beta)
                inners.append(inner)
                totals.append(total)
            ws = [None] * KEY_GROUP
            for g in reversed(range(KEY_GROUP)):
                w = jnp.exp(log_betas[g] + inners[g] + run)
                if diagonal:
                    w = jnp.where(visible[g], w, 0.0)
                ws[g] = w.astype(BF16)
                run = run + totals[g]
            new_runs.append(run)
            new_accs.append(acc + _dot(jnp.concatenate(ws, axis=1), vrows))
        return new_runs, new_accs

    zero = jnp.zeros((rows, LANES), F32)
    runs, accs = group(top, [zero, zero], [zero, zero], True)

    def body(t, carry):
        r, a = group(top - 1 - t, carry[0:2], carry[2:4], False)
        return (r[0], r[1], a[0], a[1])

    carry = lax.fori_loop(0, top, body, (runs[0], runs[1], accs[0], accs[1]))
    lane = lax.broadcasted_iota(jnp.int32, (1, LANES), 1)
    return jnp.where(lane < SB_HEAD_DIM, carry[2], carry[3])


def _sb_rows_static(i, rows, qts, ks, vs, suffix):
    nk = (i + 1) * LANES
    row = lax.broadcasted_iota(jnp.int32, (rows, LANES), 0)
    col = lax.broadcasted_iota(jnp.int32, (rows, LANES), 1)
    vis = col < row
    kall = ks[0:nk, :]
    vall = vs[0:nk, :]
    outs = []
    for qt in qts:
        z = _dot_nt(qt, kall)
        log_beta = _log_sigmoid(z)
        log_keep = log_beta - z
        sums = []
        for g in range(i + 1):
            lk = log_keep[:, g * LANES:(g + 1) * LANES]
            if g == i:
                lk = jnp.where(vis, lk, 0.0)
            sums.append(_suffix_sums(lk, suffix))
        run = jnp.zeros((rows, LANES), F32)
        ws = [None] * (i + 1)
        for g in reversed(range(i + 1)):
            w = jnp.exp(log_beta[:, g * LANES:(g + 1) * LANES] + sums[g][0] + run)
            if g == i:
                w = jnp.where(vis, w, 0.0)
            ws[g] = w.astype(BF16)
            run = run + sums[g][1]
        outs.append(_dot(jnp.concatenate(ws, axis=1), vall))
    lane = lax.broadcasted_iota(jnp.int32, (1, LANES), 1)
    return jnp.where(lane < SB_HEAD_DIM, outs[0], outs[1])


def _suffix_matrix():
    r = lax.broadcasted_iota(jnp.int32, (2 * LANES, 2 * LANES), 0)
    c = lax.broadcasted_iota(jnp.int32, (2 * LANES, 2 * LANES), 1)
    later = jnp.bitwise_and(r, LANES - 1) > c
    return jnp.where((c >= LANES) | later, 1.0, 0.0).astype(BF16)


def _suffix_sums(log_keep, suffix):
    both = _dot(_hi_lo(log_keep), suffix)
    return both[:, 0:LANES], both[:, LANES:]


def _sb_prompt_kernel(q_ref, k_ref, v_ref, o_ref, qas, qbs, ks, vs, run_ref, acc_ref, *, seq):
    n_full, tail = divmod(seq, LANES)
    lp = ks.shape[0]
    qa, qb = _split_heads(q_ref[0])
    qas[...] = qa
    qbs[...] = qb
    ks[0:seq, :] = k_ref[0].astype(BF16)
    vs[0:seq, :] = v_ref[0].astype(BF16)
    if lp > seq:
        ks[seq:lp, :] = jnp.zeros((lp - seq, LANES), BF16)
        vs[seq:lp, :] = jnp.zeros((lp - seq, LANES), BF16)
    suffix = _suffix_matrix()

    row = lax.broadcasted_iota(jnp.int32, (LANES, LANES), 0)
    col = lax.broadcasted_iota(jnp.int32, (LANES, LANES), 1)
    vis = col < row
    lane = lax.broadcasted_iota(jnp.int32, (1, LANES), 1)

    def super_tile(a, carry):
        r_base = pl.multiple_of(a * SUPER_ROWS, SUPER_ROWS)
        zero = jnp.zeros((LANES, LANES), F32)
        for tq in range(QUERY_GROUP):
            for h in range(2):
                run_ref[tq, h] = zero
                acc_ref[tq, h] = zero

        def group(grp, diagonal):
            base = pl.multiple_of(grp * GROUP_KEYS, GROUP_KEYS)
            vrows = vs[pl.ds(base, GROUP_KEYS), :]
            for h, qs in enumerate((qas, qbs)):
                qsup = qs[pl.ds(r_base, SUPER_ROWS), :]
                log_betas = [[None] * KEY_GROUP for _ in range(QUERY_GROUP)]
                inners = [[None] * KEY_GROUP for _ in range(QUERY_GROUP)]
                totals = [[None] * KEY_GROUP for _ in range(QUERY_GROUP)]
                for g in range(KEY_GROUP):
                    first = g if diagonal else 0
                    z = _dot_nt(qsup[first * LANES:, :], ks[pl.ds(base + g * LANES, LANES), :])
                    log_beta = _log_sigmoid(z)
                    log_keep = log_beta - z
                    if diagonal:
                        top = jnp.where(vis, log_keep[0:LANES], 0.0)
                        log_keep = top if first == QUERY_GROUP - 1 else jnp.concatenate([top, log_keep[LANES:]], axis=0)
                    inner, total = _suffix_sums(log_keep, suffix)
                    for tq in range(first, QUERY_GROUP):
                        s0 = (tq - first) * LANES
                        log_betas[tq][g] = log_beta[s0:s0 + LANES]
                        inners[tq][g] = inner[s0:s0 + LANES]
                        totals[tq][g] = total[s0:s0 + LANES]
                weights = []
                for tq in range(QUERY_GROUP):
                    run = run_ref[tq, h]
                    last = tq if diagonal else KEY_GROUP - 1
                    ws = [None] * (last + 1)
                    for g in reversed(range(last + 1)):
                        w = jnp.exp(log_betas[tq][g] + inners[tq][g] + run)
                        if diagonal and g == tq:
                            w = jnp.where(vis, w, 0.0)
                        ws[g] = w.astype(BF16)
                        run = run + totals[tq][g]
                    run_ref[tq, h] = run
                    wrow = ws[0] if last == 0 else jnp.concatenate(ws, axis=1)
                    if diagonal:
                        acc_ref[tq, h] += _dot(wrow, vrows[0:(last + 1) * LANES, :])
                    else:
                        weights.append(wrow)
                if not diagonal:
                    pv = _dot(jnp.concatenate(weights, axis=0), vrows)
                    for tq in range(QUERY_GROUP):
                        acc_ref[tq, h] += pv[tq * LANES:(tq + 1) * LANES]

        group(a, True)

        def lower(t, c):
            group(a - 1 - t, False)
            return c

        lax.fori_loop(0, a, lower, 0)
        for tq in range(QUERY_GROUP):
            o = jnp.where(lane < SB_HEAD_DIM, acc_ref[tq, 0], acc_ref[tq, 1])
            o_ref[0, pl.ds(r_base + tq * LANES, LANES), :] = o.astype(o_ref.dtype)
        return carry

    def full_tile(i, carry):
        r0 = pl.multiple_of(i * LANES, LANES)
        o = _sb_tile(i, LANES, (qas[pl.ds(r0, LANES), :], qbs[pl.ds(r0, LANES), :]), ks, vs, suffix)
        o_ref[0, pl.ds(r0, LANES), :] = o.astype(o_ref.dtype)
        return carry

    n_super = n_full // QUERY_GROUP
    lax.fori_loop(0, n_super, super_tile, 0)
    lax.fori_loop(n_super * QUERY_GROUP, n_full, full_tile, 0)
    if tail:
        r0 = n_full * LANES
        o = _sb_rows_static(n_full, tail, (qas[r0:r0 + tail, :], qbs[r0:r0 + tail, :]), ks, vs, suffix)
        o_ref[0, r0:r0 + tail, :] = o.astype(o_ref.dtype)


def _sb_prompt(q, k, v):
    b, seq, _ = q.shape
    assert seq % BF16_ROWS == 0
    lp = pl.cdiv(pl.cdiv(seq, LANES), KEY_GROUP) * GROUP_KEYS
    blk = lambda: pl.BlockSpec((1, seq, LANES), lambda i, p: (i, 0, p))
    return pl.pallas_call(
        functools.partial(_sb_prompt_kernel, seq=seq),
        grid=(b, SB_HEADS // 2),
        in_specs=[blk(), blk(), blk()],
        out_specs=blk(),
        out_shape=jax.ShapeDtypeStruct((b, seq, SB_HEADS * SB_HEAD_DIM), BF16),
        scratch_shapes=[pltpu.VMEM((seq, LANES), BF16), pltpu.VMEM((seq, LANES), BF16),
                        pltpu.VMEM((lp, LANES), BF16), pltpu.VMEM((lp, LANES), BF16),
                        pltpu.VMEM((QUERY_GROUP, 2, LANES, LANES), F32),
                        pltpu.VMEM((QUERY_GROUP, 2, LANES, LANES), F32)],
        compiler_params=_params(("arbitrary", "arbitrary")),
        name="stick_prompt",
    )(q, k, v)


def _sb_sample_kernel(q_ref, k_ref, v_ref, ck_ref, cv_ref, o_ref, ks, vs, *, past, n_new):
    lw = ks.shape[0]
    tot = past + n_new
    ks[0:past, :] = ck_ref[0, 0].astype(BF16)
    vs[0:past, :] = cv_ref[0, 0].astype(BF16)
    ks[past:tot, :] = k_ref[0].astype(BF16)
    vs[past:tot, :] = v_ref[0].astype(BF16)
    if lw > tot:
        ks[tot:lw, :] = jnp.zeros((lw - tot, LANES), BF16)
        vs[tot:lw, :] = jnp.zeros((lw - tot, LANES), BF16)
    o = _sb_rows_static(past // LANES, n_new, _split_heads(q_ref[0]), ks, vs, _suffix_matrix())
    o_ref[0] = o.astype(o_ref.dtype)


def _sb_sample(q, k, v, cache_k, cache_v, layer):
    b, n_new, _ = q.shape
    past = cache_k.shape[2]
    assert past % LANES == 0 and n_new <= LANES and n_new % BF16_ROWS == 0
    lw = (past // GROUP_KEYS + 1) * GROUP_KEYS
    new = lambda: pl.BlockSpec((1, n_new, LANES), lambda i, p: (i, 0, p))
    old = lambda: pl.BlockSpec((1, 1, past, LANES), lambda i, p: (layer, i, 0, p))
    return pl.pallas_call(
        functools.partial(_sb_sample_kernel, past=past, n_new=n_new),
        grid=(b, SB_HEADS // 2),
        in_specs=[new(), new(), new(), old(), old()],
        out_specs=new(),
        out_shape=jax.ShapeDtypeStruct((b, n_new, SB_HEADS * SB_HEAD_DIM), BF16),
        scratch_shapes=[pltpu.VMEM((lw, LANES), BF16), pltpu.VMEM((lw, LANES), BF16)],
        compiler_params=_params(("arbitrary", "arbitrary")),
        name="stick_sample",
    )(q, k, v, cache_k, cache_v)


def _ssd_kernel(xbc_ref, z_ref, dt_ref, hist_ref, h0_ref, cw_ref, cb_ref, dtb_ref, alog_ref,
                dexp_ref, gn_ref, y_ref, hfin_ref, prev8, ht, *, seq):
    d_in = SSM_HEADS * SSM_HEAD_DIM
    gw = d_in // SSM_GROUPS
    n_full, tail = divmod(seq, SSD_BLOCK)
    blk = SSD_BLOCK

    r = lax.broadcasted_iota(jnp.int32, (blk, blk), 0)
    c = lax.broadcasted_iota(jnp.int32, (blk, blk), 1)
    tril = r >= c
    tri = jnp.where(tril, 1.0, 0.0).astype(BF16)
    er = lax.broadcasted_iota(jnp.int32, (LANES, d_in), 0)
    ec = lax.broadcasted_iota(jnp.int32, (LANES, d_in), 1)
    expand = jnp.where(jnp.right_shift(ec, 6) == er, 1.0, 0.0).astype(BF16)
    lane = lax.broadcasted_iota(jnp.int32, (1, LANES), 1)

    prev8[...] = hist_ref[0]
    for g in range(SSM_GROUPS):
        for j in range(gw // LANES):
            rows = h0_ref[0, g * gw + j * LANES:g * gw + (j + 1) * LANES, :]
            ht[g, :, j * LANES:(j + 1) * LANES] = rows.T

    a_neg = -jnp.exp(alog_ref[...])
    cw = cw_ref[...]

    def block(t0, rows):
        xin = xbc_ref[0, pl.ds(t0, rows), :]
        win = jnp.concatenate([prev8[...], xin], axis=0)
        conv = cw[3:4] * win
        for k in range(1, SSM_CONV):
            conv = conv + cw[3 - k:4 - k] * pltpu.roll(win, k, axis=0)
        xbc = _silu(conv[SUBLANES:] + cb_ref[...])
        prev8[...] = xin[rows - SUBLANES:rows]
        dt = _softplus(dt_ref[0, pl.ds(t0, rows), :] + dtb_ref[...])
        z = z_ref[0, pl.ds(t0, rows), :]
        if rows < blk:
            pad = blk - rows
            xbc = jnp.concatenate([xbc, jnp.zeros((pad, xbc.shape[1]), F32)], axis=0)
            dt = jnp.concatenate([dt, jnp.zeros((pad, LANES), F32)], axis=0)
            z = jnp.concatenate([z, jnp.zeros((pad, d_in), F32)], axis=0)
        x = xbc[:, 0:d_in]
        cs = _dot_exact_lhs(tri, dt * a_neg)
        cs_t = cs.T
        dt_e = _dot_exact_rhs(dt, expand)
        cs_e = _dot_exact_rhs(cs, expand)
        xdt = x * dt_e
        xdt_b = xdt.astype(BF16)
        grow = jnp.exp(cs_e)
        cs_last = cs_e[blk - 1:blk, :]
        to_end = (jnp.exp(cs_last - cs_e) * xdt).astype(BF16)
        carry_decay = jnp.exp(cs_last)
        ys = []
        for g in range(SSM_GROUPS):
            bmat = xbc[:, d_in + g * SSM_D_STATE:d_in + (g + 1) * SSM_D_STATE]
            cmat = xbc[:, d_in + (SSM_GROUPS + g) * SSM_D_STATE:d_in + (SSM_GROUPS + g + 1) * SSM_D_STATE]
            cmat_b = cmat.astype(BF16)
            cb = _dot_nt(cmat_b, bmat.astype(BF16))
            hprev = ht[g]
            y_g = _dot(cmat_b, hprev.astype(BF16)) * grow[:, g * gw:(g + 1) * gw]
            parts = []
            for pr in range(gw // LANES):
                l0 = g * gw + pr * LANES
                slab = xdt_b[:, l0:l0 + LANES]
                hd = l0 // SSM_HEAD_DIM
                res = []
                for hh in (hd, hd + 1):
                    decay = jnp.exp(jnp.where(tril, cs[:, hh:hh + 1] - cs_t[hh:hh + 1, :], -jnp.inf))
                    res.append(_dot((cb * decay).astype(BF16), slab))
                parts.append(jnp.where(lane < SSM_HEAD_DIM, res[0], res[1]))
            y_g = y_g + jnp.concatenate(parts, axis=1)
            ht[g] = carry_decay[:, g * gw:(g + 1) * gw] * hprev + _dot(bmat.T.astype(BF16), to_end[:, g * gw:(g + 1) * gw])
            ys.append(y_g)
        y = jnp.concatenate(ys, axis=1)
        y = (y + dexp_ref[...] * x) * _silu(z)
        outs = []
        for g in range(SSM_GROUPS):
            outs.append(_rms(y[:, g * gw:(g + 1) * gw], gn_ref[:, g * gw:(g + 1) * gw]))
        y = jnp.concatenate(outs, axis=1)
        y_ref[0, pl.ds(t0, rows), :] = y[0:rows].astype(y_ref.dtype)

    def full_block(i, carry):
        block(pl.multiple_of(i * blk, blk), blk)
        return carry

    lax.fori_loop(0, n_full, full_block, 0)
    if tail:
        block(n_full * blk, tail)
    for g in range(SSM_GROUPS):
        for j in range(gw // LANES):
            hfin_ref[0, g * gw + j * LANES:g * gw + (j + 1) * LANES, :] = ht[g, :, j * LANES:(j + 1) * LANES].T


def _ssd(xbc, z, dt, hist8, h0, cw, cb, dtb, alog, dexp, gn):
    b, seq, cdim = xbc.shape
    d_in = SSM_HEADS * SSM_HEAD_DIM
    assert seq % SUBLANES == 0 and (seq % SSD_BLOCK) % BF16_ROWS == 0
    per_b = lambda n: pl.BlockSpec((1, seq, n), lambda i: (i, 0, 0))
    const = lambda shape: pl.BlockSpec(shape, lambda i: (0,) * len(shape))
    return pl.pallas_call(
        functools.partial(_ssd_kernel, seq=seq),
        grid=(b,),
        in_specs=[per_b(cdim), per_b(d_in), per_b(LANES),
                  pl.BlockSpec((1, SUBLANES, cdim), lambda i: (i, 0, 0)),
                  pl.BlockSpec((1, d_in, SSM_D_STATE), lambda i: (i, 0, 0)),
                  const((SSM_CONV, cdim)), const((1, cdim)), const((1, LANES)), const((1, LANES)),
                  const((1, d_in)), const((1, d_in))],
        out_specs=[per_b(d_in), pl.BlockSpec((1, d_in, SSM_D_STATE), lambda i: (i, 0, 0))],
        out_shape=[jax.ShapeDtypeStruct((b, seq, d_in), BF16),
                   jax.ShapeDtypeStruct((b, d_in, SSM_D_STATE), F32)],
        scratch_shapes=[pltpu.VMEM((SUBLANES, cdim), F32),
                        pltpu.VMEM((SSM_GROUPS, SSM_D_STATE, d_in // SSM_GROUPS), F32)],
        compiler_params=_params(("arbitrary",)),
        name="ssd",
    )(xbc, z, dt, hist8, h0, cw, cb, dtb, alog, dexp, gn)


def _out_ffn_kernel(x_ref, da_ref, y_ref, sb_ref, wo_ref, g_ref, wg_ref, wu_ref, wd_ref,
                    fw_ref, fb_ref, init_ref, o_ref, last_ref, act_ref):
    ts = x_ref.shape[1]
    d_ff = wg_ref.shape[1]
    nd = da_ref.shape[2]
    ny = y_ref.shape[2]

    @pl.when(pl.program_id(1) == 0)
    def _():
        last_ref[...] = init_ref[...]

    xr = (x_ref[0] + _dot(da_ref[0], wo_ref[0:nd, :]) + _dot(y_ref[0], wo_ref[nd:nd + ny, :])
          + _dot(sb_ref[0], wo_ref[nd + ny:, :]))
    h2 = _rms(xr, g_ref[...]).astype(BF16)
    row8 = lax.broadcasted_iota(jnp.int32, (SUBLANES, FF_CHUNK), 0)
    for c0 in range(0, d_ff, FF_CHUNK):
        cols = slice(c0, c0 + FF_CHUNK)
        g = _dot(h2, wg_ref[:, cols])
        u = _dot(h2, wu_ref[:, cols])
        hist = last_ref[0, :, cols]
        conv = fw_ref[2:3, cols] * g + fb_ref[:, cols]
        for k in (1, 2):
            shifted = pltpu.roll(g, k, axis=0)
            top = jnp.where(row8 < k, pltpu.roll(hist, k, axis=0), shifted[0:SUBLANES])
            shifted = jnp.concatenate([top, shifted[SUBLANES:]], axis=0)
            conv = conv + fw_ref[2 - k:3 - k, cols] * shifted
        last_ref[0, :, cols] = g[ts - SUBLANES:ts]
        act_ref[:, cols] = (_silu(conv) * u).astype(BF16)
    o_ref[0] = xr + _dot(act_ref[...], wd_ref[...])


def _out_ffn(x, da_o, y, sb_o, wo, g, wg, wu, wd, fw, fb, init8):
    b, seq, d = x.shape
    d_ff = wg.shape[1]
    assert d_ff % FF_CHUNK == 0
    ts = _pick_tile(seq, 704, BF16_ROWS)
    row = lambda n: pl.BlockSpec((1, ts, n), lambda i, j: (i, j, 0))
    state = lambda: pl.BlockSpec((1, SUBLANES, d_ff), lambda i, j: (i, 0, 0))
    return pl.pallas_call(
        _out_ffn_kernel,
        grid=(b, seq // ts),
        in_specs=[row(d), row(da_o.shape[2]), row(y.shape[2]), row(sb_o.shape[2]),
                  _resident(wo.shape), _resident((1, d)), _resident(wg.shape), _resident(wu.shape),
                  _resident(wd.shape), _resident(fw.shape), _resident((1, d_ff)), state()],
        out_specs=[row(d), state()],
        out_shape=[jax.ShapeDtypeStruct((b, seq, d), F32),
                   jax.ShapeDtypeStruct((b, SUBLANES, d_ff), F32)],
        scratch_shapes=[pltpu.VMEM((ts, d_ff), BF16)],
        compiler_params=_params(("arbitrary", "arbitrary")),
        name="out_ffn",
    )(x, da_o, y, sb_o, wo, g, wg, wu, wd, fw, fb, init8)


def _final_kernel(x_ref, g_ref, o_ref, *, skip):
    n = o_ref.shape[1]
    o_ref[0] = _rms(x_ref[0, skip:skip + n, :], g_ref[...])


def _final_norm(x, g, skip):
    b, seq, d = x.shape
    n = seq - skip
    assert skip % SUBLANES == 0
    return pl.pallas_call(
        functools.partial(_final_kernel, skip=skip),
        grid=(b,),
        in_specs=[pl.BlockSpec((1, seq, d), lambda i: (i, 0, 0)), pl.BlockSpec((1, d), lambda i: (0, 0))],
        out_specs=pl.BlockSpec((1, n, d), lambda i: (i, 0, 0)),
        out_shape=jax.ShapeDtypeStruct((b, n, d), F32),
        compiler_params=_params(("arbitrary",)),
        name="final_norm",
    )(x, g)


def _pad_rows_top(a, rows):
    return jnp.pad(a, ((0, 0), (rows - a.shape[1], 0), (0, 0)))


def _layer(x, past, layer, p, lam_init, bias_prompt, bias_sample):
    b, seq, d = x.shape
    proj = _in_proj(x.reshape(b * seq, d), p["norm_mix"], p["w_in"])
    da_q, da_k, da_v, ssm_z, ssm_xbc, sb_q, sb_k, sb_v, ssm_dt = [a.reshape(b, seq, -1) for a in proj]
    if past is None:
        da_o = _da_prompt(da_q, da_k, da_v, bias_prompt, p["lamp"], p["da_subln"], lam_init)
        sb_o = _sb_prompt(sb_q, sb_k, sb_v)
        hist8 = jnp.zeros((b, SUBLANES, ssm_xbc.shape[2]), F32)
        h0 = jnp.zeros((b, SSM_HEADS * SSM_HEAD_DIM, SSM_D_STATE), F32)
        ffn_init = jnp.zeros((b, SUBLANES, p["w_gate"].shape[1]), F32)
    else:
        c_da_k, c_da_v, c_sb_k, c_sb_v, state_ssm, state_conv, state_ffn = past
        da_o = _da_sample(da_q, da_k, da_v, c_da_k, c_da_v, layer, bias_sample, p["lamp"], p["da_subln"], lam_init)
        sb_o = _sb_sample(sb_q, sb_k, sb_v, c_sb_k, c_sb_v, layer)
        hist8 = _pad_rows_top(state_conv[layer], SUBLANES)
        h0 = state_ssm[layer].reshape(b, SSM_HEADS * SSM_HEAD_DIM, SSM_D_STATE)
        ffn_init = _pad_rows_top(state_ffn[layer], SUBLANES)
    y, h_final = _ssd(ssm_xbc, ssm_z, ssm_dt, hist8, h0, p["conv_w"], p["conv_b"], p["dt_bias"],
                      p["a_log"], p["d_exp"], p["ssm_norm"])
    x_new, ffn_last = _out_ffn(x, da_o, y, sb_o, p["w_out"], p["norm_ffn"], p["w_gate"], p["w_up"],
                               p["w_down"], p["fconv_w"], p["fconv_b"], ffn_init)
    conv_in_tail = jnp.concatenate([hist8, ssm_xbc[:, -min(seq, SUBLANES):]], axis=1)
    states = (da_k.reshape(b, seq, DA_HEADS, 2, DA_HEAD_DIM),
              da_v.reshape(b, seq, DA_HEADS, DA_VDIM),
              sb_k.reshape(b, seq, SB_HEADS, SB_HEAD_DIM),
              sb_v.reshape(b, seq, SB_HEADS, SB_HEAD_DIM),
              h_final.reshape(b, SSM_HEADS, SSM_HEAD_DIM, SSM_D_STATE),
              conv_in_tail[:, -(SSM_CONV - 1):],
              ffn_last[:, -(FFN_CONV - 1):])
    return x_new, states


def _pad_lanes(v, n):
    return jnp.pad(v, (0, n - v.shape[0])).reshape(1, n)


def kernel(x_prompt, x_sample, cache_da_k, cache_da_v, cache_sb_k, cache_sb_v, state_ssm, state_ssm_conv, state_ffn_conv, meta_tokens, rel_bias_table, w_in, w_out, norm_mix, norm_ffn, da_lambda_q1, da_lambda_k1, da_lambda_q2, da_lambda_k2, da_subln, ssm_conv_w, ssm_conv_b, ssm_dt_bias, ssm_a_log, ssm_d, ssm_norm, ffn_w_gate, ffn_w_up, ffn_w_down, ffn_conv_w, ffn_conv_b, final_norm):
    depth = w_in.shape[0]
    b = x_prompt.shape[0]
    d = x_prompt.shape[2]
    sb, n_new, _ = x_sample.shape
    past_len = cache_da_k.shape[2]
    n_meta = meta_tokens.shape[0]
    assert n_meta == N_META

    xp = jnp.concatenate([jnp.broadcast_to(meta_tokens[None], (b, n_meta, d)), x_prompt], axis=1)
    xs = x_sample
    bias_prompt, bias_sample = _bias_tiles(rel_bias_table, past_len, n_new)
    cache = (cache_da_k.reshape(depth, sb, past_len, -1), cache_da_v.reshape(depth, sb, past_len, -1),
             cache_sb_k.reshape(depth, sb, past_len, -1), cache_sb_v.reshape(depth, sb, past_len, -1),
             state_ssm, state_ssm_conv, state_ffn_conv)

    p_states, s_states = [], []
    for i in range(depth):
        wi = w_in[i]
        w_r = jnp.concatenate([wi[:, :_OFF_DT], wi[:, _OFF_DT + SSM_HEADS:],
                               jnp.pad(wi[:, _OFF_DT:_OFF_DT + SSM_HEADS], ((0, 0), (0, LANES - SSM_HEADS)))],
                              axis=1).astype(BF16)
        p = dict(
            w_in=w_r, w_out=w_out[i].astype(BF16),
            norm_mix=norm_mix[i].reshape(1, d), norm_ffn=norm_ffn[i].reshape(1, d),
            lamp=jnp.stack([da_lambda_q1[i], da_lambda_k1[i], da_lambda_q2[i], da_lambda_k2[i]]),
            da_subln=da_subln[i].reshape(1, DA_VDIM),
            conv_w=ssm_conv_w[i], conv_b=ssm_conv_b[i].reshape(1, -1),
            dt_bias=_pad_lanes(ssm_dt_bias[i], LANES), a_log=_pad_lanes(ssm_a_log[i], LANES),
            d_exp=jnp.repeat(ssm_d[i], SSM_HEAD_DIM).reshape(1, -1), ssm_norm=ssm_norm[i].reshape(1, -1),
            w_gate=ffn_w_gate[i].astype(BF16), w_up=ffn_w_up[i].astype(BF16), w_down=ffn_w_down[i].astype(BF16),
            fconv_w=ffn_conv_w[i], fconv_b=ffn_conv_b[i].reshape(1, -1),
        )
        lam_init = 0.8 - 0.6 * math.exp(-0.3 * i)
        xp, st_p = _layer(xp, None, i, p, lam_init, bias_prompt, bias_sample)
        xs, st_s = _layer(xs, cache, i, p, lam_init, bias_prompt, bias_sample)
        p_states.append(st_p)
        s_states.append(st_s)

    g_fin = final_norm.reshape(1, d)
    y_prompt = _final_norm(xp, g_fin, n_meta)
    y_sample = _final_norm(xs, g_fin, 0)
    stack = lambda states: tuple(jnp.stack([s[j] for s in states], axis=0) for j in range(7))
    return (y_prompt, y_sample) + stack(p_states) + stack(s_states)
```

```python
import functools
import math

import jax
import jax.numpy as jnp
from jax import lax
from jax.experimental import pallas as pl
from jax.experimental.pallas import tpu as pltpu

F32 = jnp.float32
BF16 = jnp.bfloat16

LANES = 128
SUBLANES = 8
BF16_ROWS = 16
V7X_VMEM_BYTES = 64 * 1024 * 1024
VMEM_LIMIT = 56 * 1024 * 1024

CHUNK = 64
N_META = 16
EPS = 1e-6
NEG_INF = -1e30
DA_HEADS = 4
DA_HEAD_DIM = 64
DA_VDIM = 128
SB_HEADS = 8
SB_HEAD_DIM = 64
SSM_HEADS = 8
SSM_HEAD_DIM = 64
SSM_GROUPS = 2
SSM_D_STATE = 128
SSM_CONV = 4
FFN_CONV = 3
REL_BUCKETS = 32
REL_MAX_DIST = 128
QK_SCALE = 0.125
SSD_BLOCK = 128
FF_CHUNK = 256
KEY_GROUP = 4
GROUP_KEYS = KEY_GROUP * LANES
QUERY_GROUP = KEY_GROUP
SUPER_ROWS = QUERY_GROUP * LANES
CACHE_SLABS = 4
BIAS_HIDDEN = 4


def _dot(a, b):
    return jnp.dot(a, b, preferred_element_type=F32)


def _dot_nt(a, b):
    return lax.dot_general(a, b, (((1,), (1,)), ((), ())), preferred_element_type=F32)


def _split3(x):
    hi = x.astype(BF16)
    r1 = x - hi.astype(F32)
    mid = r1.astype(BF16)
    lo = (r1 - mid.astype(F32)).astype(BF16)
    return hi, mid, lo


def _dot_exact_rhs(x, m_bf16):
    hi, mid, lo = _split3(x)
    return _dot(hi, m_bf16) + _dot(mid, m_bf16) + _dot(lo, m_bf16)


def _dot_exact_lhs(m_bf16, x):
    hi, mid, lo = _split3(x)
    return _dot(m_bf16, hi) + _dot(m_bf16, mid) + _dot(m_bf16, lo)


def _rms(x, g):
    return x * lax.rsqrt(jnp.mean(x * x, axis=-1, keepdims=True) + EPS) * g


def _silu(x):
    return x * (1.0 / (1.0 + jnp.exp(-x)))


def _softplus(x):
    return jnp.maximum(x, 0.0) + jnp.log1p(jnp.exp(-jnp.abs(x)))


def _pick_tile(n, cap, mult):
    best = None
    for t in range(mult, min(n, cap) + 1, mult):
        if n % t == 0:
            best = t
    assert best is not None, (n, cap, mult)
    return best


def _resident(shape):
    nd = len(shape)
    return pl.BlockSpec(shape, lambda *_: (0,) * nd, pipeline_mode=pl.Buffered(1))


def _params(sem):
    return pltpu.CompilerParams(dimension_semantics=sem, vmem_limit_bytes=VMEM_LIMIT)


def _rel_bucket(rel):
    half = REL_BUCKETS // 2
    max_exact = half // 2
    n = jnp.abs(rel)
    nf = jnp.maximum(n, 1).astype(jnp.float32)
    large = max_exact + (jnp.log(nf / max_exact) / math.log(REL_MAX_DIST / max_exact)
                         * (half - max_exact)).astype(jnp.int32)
    large = jnp.minimum(large, half - 1)
    return jnp.where(rel > 0, half, 0) + jnp.where(n < max_exact, n, large)


def _bias_kernel(tab_ref, bp_ref, bs_ref, op_ref, os_ref, *, past, n_new):
    h = pl.program_id(0)

    def lookup(bucket):
        val = jnp.zeros(bucket.shape, F32)
        for b in range(REL_BUCKETS):
            val = jnp.where(bucket == b, tab_ref[b, h], val)
        return val

    r = lax.broadcasted_iota(jnp.int32, (LANES, LANES), 0)
    c = lax.broadcasted_iota(jnp.int32, (LANES, LANES), 1)
    qc = jnp.right_shift(r - N_META, 6)
    kc = jnp.right_shift(c - N_META, 6)
    for slot in range(BIAS_HIDDEN):
        val = lookup(bp_ref[slot])
        if slot > 0:
            d = slot - 2
            val = jnp.where(2 * d + kc - qc <= 0, val, NEG_INF)
        op_ref[0, slot] = val
    op_ref[0, BIAS_HIDDEN] = jnp.full((LANES, LANES), NEG_INF, F32)

    lw = bs_ref.shape[1]
    rs = lax.broadcasted_iota(jnp.int32, (n_new, lw), 0) + past
    cs = lax.broadcasted_iota(jnp.int32, (n_new, lw), 1)
    vis = (jnp.right_shift(cs, 6) <= jnp.right_shift(rs, 6)) & (cs < past + n_new)
    os_ref[0] = jnp.where(vis, lookup(bs_ref[...]), NEG_INF)


def _bias_tiles(rel_table, past, n_new):
    lw = pl.cdiv(past + n_new, LANES) * LANES
    r = jnp.arange(LANES, dtype=jnp.int32)[:, None]
    c = jnp.arange(LANES, dtype=jnp.int32)[None, :]
    far = jnp.full((LANES, LANES), -2 * LANES, jnp.int32)
    bp = jnp.stack([_rel_bucket(far)] + [_rel_bucket(d * LANES + c - r) for d in (-1, 0, 1)])
    qs = past + jnp.arange(n_new, dtype=jnp.int32)[:, None]
    ks = jnp.arange(lw, dtype=jnp.int32)[None, :]
    bs = _rel_bucket(ks - qs)
    return pl.pallas_call(
        functools.partial(_bias_kernel, past=past, n_new=n_new),
        grid=(DA_HEADS,),
        in_specs=[pl.BlockSpec(memory_space=pltpu.SMEM),
                  pl.BlockSpec((4, LANES, LANES), lambda h: (0, 0, 0)),
                  pl.BlockSpec((n_new, lw), lambda h: (0, 0))],
        out_specs=[pl.BlockSpec((1, BIAS_HIDDEN + 1, LANES, LANES), lambda h: (h, 0, 0, 0)),
                   pl.BlockSpec((1, n_new, lw), lambda h: (h, 0, 0))],
        out_shape=[jax.ShapeDtypeStruct((DA_HEADS, BIAS_HIDDEN + 1, LANES, LANES), F32),
                   jax.ShapeDtypeStruct((DA_HEADS, n_new, lw), F32)],
        name="rel_bias_tiles",
    )(rel_table, bp, bs)


_SEG = (("da_q", 512, BF16), ("da_k", 512, F32), ("da_v", 512, F32), ("ssm_z", 512, F32),
        ("ssm_xbc", 1024, F32), ("sb_q", 512, BF16), ("sb_k", 512, F32), ("sb_v", 512, F32),
        ("ssm_dt", LANES, F32))
_N_PROJ = sum(s[1] for s in _SEG)
_OFF_DT = 3072


def _inproj_kernel(x_ref, g_ref, w_ref, *out_refs):
    h = _rms(x_ref[...], g_ref[...]).astype(BF16)
    off = 0
    for (_, n, dt), o_ref in zip(_SEG, out_refs):
        o_ref[...] = _dot(h, w_ref[:, off:off + n]).astype(dt)
        off += n


def _in_proj(x2d, g, w_r):
    t, d = x2d.shape
    tm = _pick_tile(t, 512, BF16_ROWS)
    return pl.pallas_call(
        _inproj_kernel,
        grid=(t // tm,),
        in_specs=[pl.BlockSpec((tm, d), lambda i: (i, 0)), _resident((1, d)), _resident((d, _N_PROJ))],
        out_specs=[pl.BlockSpec((tm, n), lambda i: (i, 0)) for _, n, _ in _SEG],
        out_shape=[jax.ShapeDtypeStruct((t, n), dt) for _, n, dt in _SEG],
        compiler_params=_params(("arbitrary",)),
        name="in_proj",
    )(x2d, g, w_r)


def _lambda(lamp_ref, lam_init):
    lp = lamp_ref[...]
    return (jnp.exp(jnp.sum(lp[0:1] * lp[1:2], axis=1, keepdims=True))
            - jnp.exp(jnp.sum(lp[2:3] * lp[3:4], axis=1, keepdims=True)) + lam_init)


def _split_maps(q_bf16):
    lane = lax.broadcasted_iota(jnp.int32, (1, LANES), 1)
    qf = q_bf16.astype(F32) * QK_SCALE
    return (jnp.where(lane < DA_HEAD_DIM, qf, 0.0).astype(BF16),
            jnp.where(lane >= DA_HEAD_DIM, qf, 0.0).astype(BF16))


def _da_prompt_kernel(q_ref, k_ref, v_ref, bm_ref, lamp_ref, subln_ref, o_ref,
                      q1s, q2s, ks, vs, sbuf, sbig, mx_ref, l_ref, acc_ref, *, seq, lam_init):
    n_full, tail = divmod(seq, LANES)
    n_blk = n_full + (1 if tail else 0)
    lp = ks.shape[0]
    q1, q2 = _split_maps(q_ref[0])
    q1s[...] = q1
    q2s[...] = q2
    ks[0:seq, :] = k_ref[0].astype(BF16)
    vs[0:seq, :] = v_ref[0].astype(BF16)
    ks[seq:lp, :] = jnp.zeros((lp - seq, LANES), BF16)
    vs[seq:lp, :] = jnp.zeros((lp - seq, LANES), BF16)
    lam = _lambda(lamp_ref, lam_init)
    gsub = subln_ref[...] * (1.0 - lam_init)

    def tile(i, rows, r0):
        q1t = q1s[pl.ds(r0, rows), :]
        q2t = q2s[pl.ds(r0, rows), :]
        n_grp = (jnp.minimum(i + 2, n_blk) + KEY_GROUP - 1) // KEY_GROUP

        def scores(grp, carry):
            m1, m2 = carry
            base = pl.multiple_of(grp * GROUP_KEYS, GROUP_KEYS)
            for g in range(KEY_GROUP):
                kb = grp * KEY_GROUP + g
                kblk = ks[pl.ds(base + g * LANES, LANES), :]
                slot = jnp.where(kb >= n_blk, BIAS_HIDDEN, jnp.clip(kb - i + 2, 0, BIAS_HIDDEN))
                bias = bm_ref[0, slot, 0:rows, :]
                s1 = _dot_nt(q1t, kblk) + bias
                s2 = _dot_nt(q2t, kblk) + bias
                sbuf[0, grp, 0:rows, g * LANES:(g + 1) * LANES] = s1
                sbuf[1, grp, 0:rows, g * LANES:(g + 1) * LANES] = s2
                m1 = jnp.maximum(m1, s1)
                m2 = jnp.maximum(m2, s2)
            return m1, m2

        ninf = jnp.full((rows, LANES), -jnp.inf, F32)
        m1, m2 = lax.fori_loop(0, n_grp, scores, (ninf, ninf))
        m1 = jnp.max(m1, axis=1, keepdims=True)
        m2 = jnp.max(m2, axis=1, keepdims=True)

        def exp_pv(grp, carry):
            l1, l2, a1, a2 = carry
            base = pl.multiple_of(grp * GROUP_KEYS, GROUP_KEYS)
            ps1, ps2 = [], []
            for g in range(KEY_GROUP):
                p1 = jnp.exp(sbuf[0, grp, 0:rows, g * LANES:(g + 1) * LANES] - m1)
                p2 = jnp.exp(sbuf[1, grp, 0:rows, g * LANES:(g + 1) * LANES] - m2)
                l1 = l1 + p1
                l2 = l2 + p2
                ps1.append(p1.astype(BF16))
                ps2.append(p2.astype(BF16))
            vrows = vs[pl.ds(base, GROUP_KEYS), :]
            a1 = a1 + _dot(jnp.concatenate(ps1, axis=1), vrows)
            a2 = a2 + _dot(jnp.concatenate(ps2, axis=1), vrows)
            return l1, l2, a1, a2

        zero = jnp.zeros((rows, LANES), F32)
        l1, l2, a1, a2 = lax.fori_loop(0, n_grp, exp_pv, (zero, zero, zero, zero))
        c1 = 1.0 / jnp.sum(l1, axis=1, keepdims=True)
        c2 = lam / jnp.sum(l2, axis=1, keepdims=True)
        o_ref[0, pl.ds(r0, rows), :] = _rms(a1 * c1 - a2 * c2, gsub).astype(o_ref.dtype)

    def super_tile(a, carry):
        r_base = pl.multiple_of(a * SUPER_ROWS, SUPER_ROWS)
        k_base = pl.multiple_of(a * GROUP_KEYS, GROUP_KEYS)
        maps = (q1s, q2s)
        ninf = jnp.full((LANES, LANES), -jnp.inf, F32)
        zero = jnp.zeros((LANES, LANES), F32)
        for m in range(2):
            for tq in range(QUERY_GROUP):
                mx_ref[m, tq] = ninf
                l_ref[m, tq] = zero
                acc_ref[m, tq] = zero

        def scores_lower(grp, c):
            base = pl.multiple_of(grp * GROUP_KEYS, GROUP_KEYS)
            for m in range(2):
                qsup = maps[m][pl.ds(r_base, SUPER_ROWS), :]
                mx = [mx_ref[m, tq] for tq in range(QUERY_GROUP)]
                for g in range(KEY_GROUP):
                    kb = grp * KEY_GROUP + g
                    s_all = _dot_nt(qsup, ks[pl.ds(base + g * LANES, LANES), :])
                    for tq in range(QUERY_GROUP):
                        slot = jnp.clip(kb - (a * QUERY_GROUP + tq) + 2, 0, BIAS_HIDDEN)
                        s = s_all[tq * LANES:(tq + 1) * LANES] + bm_ref[0, slot]
                        sbig[m, tq, kb] = s
                        mx[tq] = jnp.maximum(mx[tq], s)
                for tq in range(QUERY_GROUP):
                    mx_ref[m, tq] = mx[tq]
            return c

        lax.fori_loop(0, a, scores_lower, 0)

        for m in range(2):
            qsup = maps[m][pl.ds(r_base, SUPER_ROWS), :]
            mx = [mx_ref[m, tq] for tq in range(QUERY_GROUP)]
            for g in range(KEY_GROUP + 1):
                first = max(g - 1, 0)
                kb = a * KEY_GROUP + g
                s_all = _dot_nt(qsup[first * LANES:, :], ks[pl.ds(k_base + g * LANES, LANES), :])
                for tq in range(first, QUERY_GROUP):
                    slot = 0 if g - tq <= -2 else g - tq + 2
                    if g == KEY_GROUP:
                        slot = jnp.where(kb >= n_blk, BIAS_HIDDEN, slot)
                    s = s_all[(tq - first) * LANES:(tq - first + 1) * LANES] + bm_ref[0, slot]
                    sbig[m, tq, kb] = s
                    mx[tq] = jnp.maximum(mx[tq], s)
            for tq in range(QUERY_GROUP):
                mx_ref[m, tq] = jnp.broadcast_to(jnp.max(mx[tq], axis=1, keepdims=True), (LANES, LANES))

        def pv_lower(grp, c):
            base = pl.multiple_of(grp * GROUP_KEYS, GROUP_KEYS)
            vrows = vs[pl.ds(base, GROUP_KEYS), :]
            for m in range(2):
                prows = []
                for tq in range(QUERY_GROUP):
                    mrow = mx_ref[m, tq]
                    lsum = l_ref[m, tq]
                    ps = []
                    for g in range(KEY_GROUP):
                        p = jnp.exp(sbig[m, tq, grp * KEY_GROUP + g] - mrow)
                        lsum = lsum + p
                        ps.append(p.astype(BF16))
                    l_ref[m, tq] = lsum
                    prows.append(jnp.concatenate(ps, axis=1))
                pv = _dot(jnp.concatenate(prows, axis=0), vrows)
                for tq in range(QUERY_GROUP):
                    acc_ref[m, tq] += pv[tq * LANES:(tq + 1) * LANES]
            return c

        lax.fori_loop(0, a, pv_lower, 0)

        for tq in range(QUERY_GROUP):
            nb = tq + 2
            vrows = vs[pl.ds(k_base, nb * LANES), :]
            outs = []
            for m in range(2):
                mrow = mx_ref[m, tq]
                lsum = l_ref[m, tq]
                ps = []
                for g in range(nb):
                    p = jnp.exp(sbig[m, tq, a * KEY_GROUP + g] - mrow)
                    lsum = lsum + p
                    ps.append(p.astype(BF16))
                acc = acc_ref[m, tq] + _dot(jnp.concatenate(ps, axis=1), vrows)
                outs.append((acc, jnp.sum(lsum, axis=1, keepdims=True)))
            o = outs[0][0] * (1.0 / outs[0][1]) - outs[1][0] * (lam / outs[1][1])
            o_ref[0, pl.ds(r_base + tq * LANES, LANES), :] = _rms(o, gsub).astype(o_ref.dtype)
        return carry

    def full_tile(i, carry):
        tile(i, LANES, pl.multiple_of(i * LANES, LANES))
        return carry

    n_super = n_full // QUERY_GROUP
    lax.fori_loop(0, n_super, super_tile, 0)
    lax.fori_loop(n_super * QUERY_GROUP, n_full, full_tile, 0)
    if tail:
        r0 = n_full * LANES
        nk = n_blk * LANES
        tiles = [bm_ref[0, 0, 0:tail, :]] * max(n_full - 1, 0)
        if n_full >= 1:
            tiles.append(bm_ref[0, 1, 0:tail, :])
        tiles.append(bm_ref[0, 2, 0:tail, :])
        bias = tiles[0] if len(tiles) == 1 else jnp.concatenate(tiles, axis=1)
        kall = ks[0:nk, :]
        s1 = _dot_nt(q1s[r0:r0 + tail, :], kall) + bias
        s2 = _dot_nt(q2s[r0:r0 + tail, :], kall) + bias
        p1 = jnp.exp(s1 - jnp.max(s1, axis=1, keepdims=True))
        p2 = jnp.exp(s2 - jnp.max(s2, axis=1, keepdims=True))
        c1 = 1.0 / jnp.sum(p1, axis=1, keepdims=True)
        c2 = lam / jnp.sum(p2, axis=1, keepdims=True)
        acc = _dot((p1 * c1 - p2 * c2).astype(BF16), vs[0:nk, :])
        o_ref[0, r0:r0 + tail, :] = _rms(acc, gsub).astype(o_ref.dtype)


def _da_prompt(q, k, v, bm, lamp, subln, lam_init):
    b, seq, _ = q.shape
    assert seq % BF16_ROWS == 0 and (seq - N_META) % CHUNK == 0
    n_blk = pl.cdiv(seq, LANES)
    n_grp = pl.cdiv(n_blk + 1, KEY_GROUP)
    lp = n_grp * GROUP_KEYS
    n_big = (seq // SUPER_ROWS) * QUERY_GROUP + 1
    stat = lambda: pltpu.VMEM((2, QUERY_GROUP, LANES, LANES), F32)
    blk = lambda: pl.BlockSpec((1, seq, LANES), lambda i, h: (i, 0, h))
    return pl.pallas_call(
        functools.partial(_da_prompt_kernel, seq=seq, lam_init=lam_init),
        grid=(b, DA_HEADS),
        in_specs=[blk(), blk(), blk(),
                  pl.BlockSpec((1, BIAS_HIDDEN + 1, LANES, LANES), lambda i, h: (h, 0, 0, 0)),
                  pl.BlockSpec((4, DA_HEAD_DIM), lambda i, h: (0, 0)),
                  pl.BlockSpec((1, DA_VDIM), lambda i, h: (0, 0))],
        out_specs=blk(),
        out_shape=jax.ShapeDtypeStruct((b, seq, DA_HEADS * DA_VDIM), BF16),
        scratch_shapes=[pltpu.VMEM((seq, LANES), BF16), pltpu.VMEM((seq, LANES), BF16),
                        pltpu.VMEM((lp, LANES), BF16), pltpu.VMEM((lp, LANES), BF16),
                        pltpu.VMEM((2, n_grp, LANES, GROUP_KEYS), F32),
                        pltpu.VMEM((2, QUERY_GROUP, n_big, LANES, LANES), F32), stat(), stat(), stat()],
        compiler_params=_params(("arbitrary", "arbitrary")),
        name="diff_attn_prompt",
    )(q, k, v, bm, lamp, subln)


def _da_sample_kernel(q_ref, k_ref, v_ref, ck_ref, cv_ref, bs_ref, lamp_ref, subln_ref, o_ref,
                      ks, vs, *, past, n_new, lam_init):
    lw = ks.shape[0]
    tot = past + n_new
    slab = pl.ds(pl.program_id(1), past, stride=CACHE_SLABS)
    ks[0:past, :] = ck_ref[0, 0, slab, :].astype(BF16)
    vs[0:past, :] = cv_ref[0, 0, slab, :].astype(BF16)
    ks[past:tot, :] = k_ref[0].astype(BF16)
    vs[past:tot, :] = v_ref[0].astype(BF16)
    if lw > tot:
        ks[tot:lw, :] = jnp.zeros((lw - tot, LANES), BF16)
        vs[tot:lw, :] = jnp.zeros((lw - tot, LANES), BF16)
    q1, q2 = _split_maps(q_ref[0])
    bias = bs_ref[0]
    kall = ks[...]
    s1 = _dot_nt(q1, kall) + bias
    s2 = _dot_nt(q2, kall) + bias
    p1 = jnp.exp(s1 - jnp.max(s1, axis=1, keepdims=True))
    p2 = jnp.exp(s2 - jnp.max(s2, axis=1, keepdims=True))
    c1 = 1.0 / jnp.sum(p1, axis=1, keepdims=True)
    c2 = _lambda(lamp_ref, lam_init) / jnp.sum(p2, axis=1, keepdims=True)
    acc = _dot((p1 * c1 - p2 * c2).astype(BF16), vs[...])
    o_ref[0] = _rms(acc, subln_ref[...] * (1.0 - lam_init)).astype(o_ref.dtype)


def _da_sample(q, k, v, cache_k, cache_v, layer, bs, lamp, subln, lam_init):
    b, n_new, _ = q.shape
    past = cache_k.shape[2] // CACHE_SLABS
    assert past % BF16_ROWS == 0 and n_new % BF16_ROWS == 0
    lw = bs.shape[2]
    new = lambda: pl.BlockSpec((1, n_new, LANES), lambda i, h: (i, 0, h))
    old = lambda: pl.BlockSpec((1, 1, past * CACHE_SLABS, LANES), lambda i, h: (layer, i, 0, 0))
    return pl.pallas_call(
        functools.partial(_da_sample_kernel, past=past, n_new=n_new, lam_init=lam_init),
        grid=(b, DA_HEADS),
        in_specs=[new(), new(), new(), old(), old(),
                  pl.BlockSpec((1, n_new, lw), lambda i, h: (h, 0, 0)),
                  pl.BlockSpec((4, DA_HEAD_DIM), lambda i, h: (0, 0)),
                  pl.BlockSpec((1, DA_VDIM), lambda i, h: (0, 0))],
        out_specs=new(),
        out_shape=jax.ShapeDtypeStruct((b, n_new, DA_HEADS * DA_VDIM), BF16),
        scratch_shapes=[pltpu.VMEM((lw, LANES), BF16), pltpu.VMEM((lw, LANES), BF16)],
        compiler_params=_params(("arbitrary", "arbitrary")),
        name="diff_attn_sample",
    )(q, k, v, cache_k, cache_v, bs, lamp, subln)


def _split_heads(q_bf16):
    lane = lax.broadcasted_iota(jnp.int32, (1, LANES), 1)
    qf = q_bf16.astype(F32) * QK_SCALE
    return (jnp.where(lane < SB_HEAD_DIM, qf, 0.0).astype(BF16),
            jnp.where(lane >= SB_HEAD_DIM, qf, 0.0).astype(BF16))


def _log_sigmoid(z):
    neg_abs = lax.bitcast_convert_type(lax.bitcast_convert_type(z, jnp.uint32) | jnp.uint32(0x80000000), F32)
    return jnp.minimum(z, 0.0) - jnp.log(1.0 + jnp.exp(neg_abs))


def _hi_lo(x):
    hi = lax.bitcast_convert_type(lax.bitcast_convert_type(x, jnp.uint32) & jnp.uint32(0xFFFF0000), F32)
    return jnp.concatenate([hi.astype(BF16), (x - hi).astype(BF16)], axis=1)


def _sb_tile(i, rows, qts, ks, vs, suffix):
    row = lax.broadcasted_iota(jnp.int32, (rows, LANES), 0)
    col = lax.broadcasted_iota(jnp.int32, (rows, LANES), 1)
    col_minus_row = col - row
    top = i // KEY_GROUP

    def group(grp, runs, accs, diagonal):
        base = pl.multiple_of(grp * GROUP_KEYS, GROUP_KEYS)
        vrows = vs[pl.ds(base, GROUP_KEYS), :]
        new_runs, new_accs = [], []
        for qt, run, acc in zip(qts, runs, accs):
            log_betas, inners, totals, visible = [], [], [], []
            for g in range(KEY_GROUP):
                z = _dot_nt(qt, ks[pl.ds(base + g * LANES, LANES), :])
                log_beta = _log_sigmoid(z)
                log_keep = log_beta - z
                if diagonal:
                    vis = col_minus_row < (i - grp * KEY_GROUP - g) * LANES
                    log_keep = jnp.where(vis, log_keep, 0.0)
                    visible.append(vis)
                inner, total = _suffix_sums(log_keep, suffix)
                log_betas.append(log_beta)
                inners.append(inner)
                totals.append(total)
            ws = [None] * KEY_GROUP
            for g in reversed(range(KEY_GROUP)):
                w = jnp.exp(log_betas[g] + inners[g] + run)
                if diagonal:
                    w = jnp.where(visible[g], w, 0.0)
                ws[g] = w.astype(BF16)
                run = run + totals[g]
            new_runs.append(run)
            new_accs.append(acc + _dot(jnp.concatenate(ws, axis=1), vrows))
        return new_runs, new_accs

    zero = jnp.zeros((rows, LANES), F32)
    runs, accs = group(top, [zero, zero], [zero, zero], True)

    def body(t, carry):
        r, a = group(top - 1 - t, carry[0:2], carry[2:4], False)
        return (r[0], r[1], a[0], a[1])

    carry = lax.fori_loop(0, top, body, (runs[0], runs[1], accs[0], accs[1]))
    lane = lax.broadcasted_iota(jnp.int32, (1, LANES), 1)
    return jnp.where(lane < SB_HEAD_DIM, carry[2], carry[3])


def _sb_rows_static(i, rows, qts, ks, vs, suffix):
    nk = (i + 1) * LANES
    row = lax.broadcasted_iota(jnp.int32, (rows, LANES), 0)
    col = lax.broadcasted_iota(jnp.int32, (rows, LANES), 1)
    vis = col < row
    kall = ks[0:nk, :]
    vall = vs[0:nk, :]
    outs = []
    for qt in qts:
        z = _dot_nt(qt, kall)
        log_beta = _log_sigmoid(z)
        log_keep = log_beta - z
        sums = []
        for g in range(i + 1):
            lk = log_keep[:, g * LANES:(g + 1) * LANES]
            if g == i:
                lk = jnp.where(vis, lk, 0.0)
            sums.append(_suffix_sums(lk, suffix))
        run = jnp.zeros((rows, LANES), F32)
        ws = [None] * (i + 1)
        for g in reversed(range(i + 1)):
            w = jnp.exp(log_beta[:, g * LANES:(g + 1) * LANES] + sums[g][0] + run)
            if g == i:
                w = jnp.where(vis, w, 0.0)
            ws[g] = w.astype(BF16)
            run = run + sums[g][1]
        outs.append(_dot(jnp.concatenate(ws, axis=1), vall))
    lane = lax.broadcasted_iota(jnp.int32, (1, LANES), 1)
    return jnp.where(lane < SB_HEAD_DIM, outs[0], outs[1])


def _suffix_matrix():
    r = lax.broadcasted_iota(jnp.int32, (2 * LANES, 2 * LANES), 0)
    c = lax.broadcasted_iota(jnp.int32, (2 * LANES, 2 * LANES), 1)
    later = jnp.bitwise_and(r, LANES - 1) > c
    return jnp.where((c >= LANES) | later, 1.0, 0.0).astype(BF16)


def _suffix_sums(log_keep, suffix):
    both = _dot(_hi_lo(log_keep), suffix)
    return both[:, 0:LANES], both[:, LANES:]


def _sb_prompt_kernel(q_ref, k_ref, v_ref, o_ref, qas, qbs, ks, vs, run_ref, acc_ref, *, seq):
    n_full, tail = divmod(seq, LANES)
    lp = ks.shape[0]
    qa, qb = _split_heads(q_ref[0])
    qas[...] = qa
    qbs[...] = qb
    ks[0:seq, :] = k_ref[0].astype(BF16)
    vs[0:seq, :] = v_ref[0].astype(BF16)
    if lp > seq:
        ks[seq:lp, :] = jnp.zeros((lp - seq, LANES), BF16)
        vs[seq:lp, :] = jnp.zeros((lp - seq, LANES), BF16)
    suffix = _suffix_matrix()

    row = lax.broadcasted_iota(jnp.int32, (LANES, LANES), 0)
    col = lax.broadcasted_iota(jnp.int32, (LANES, LANES), 1)
    vis = col < row
    lane = lax.broadcasted_iota(jnp.int32, (1, LANES), 1)

    def super_tile(a, carry):
        r_base = pl.multiple_of(a * SUPER_ROWS, SUPER_ROWS)
        zero = jnp.zeros((LANES, LANES), F32)
        for tq in range(QUERY_GROUP):
            for h in range(2):
                run_ref[tq, h] = zero
                acc_ref[tq, h] = zero

        def group(grp, diagonal):
            base = pl.multiple_of(grp * GROUP_KEYS, GROUP_KEYS)
            vrows = vs[pl.ds(base, GROUP_KEYS), :]
            for h, qs in enumerate((qas, qbs)):
                qsup = qs[pl.ds(r_base, SUPER_ROWS), :]
                log_betas = [[None] * KEY_GROUP for _ in range(QUERY_GROUP)]
                inners = [[None] * KEY_GROUP for _ in range(QUERY_GROUP)]
                totals = [[None] * KEY_GROUP for _ in range(QUERY_GROUP)]
                for g in range(KEY_GROUP):
                    first = g if diagonal else 0
                    z = _dot_nt(qsup[first * LANES:, :], ks[pl.ds(base + g * LANES, LANES), :])
                    log_beta = _log_sigmoid(z)
                    log_keep = log_beta - z
                    if diagonal:
                        top = jnp.where(vis, log_keep[0:LANES], 0.0)
                        log_keep = top if first == QUERY_GROUP - 1 else jnp.concatenate([top, log_keep[LANES:]], axis=0)
                    inner, total = _suffix_sums(log_keep, suffix)
                    for tq in range(first, QUERY_GROUP):
                        s0 = (tq - first) * LANES
                        log_betas[tq][g] = log_beta[s0:s0 + LANES]
                        inners[tq][g] = inner[s0:s0 + LANES]
                        totals[tq][g] = total[s0:s0 + LANES]
                weights = []
                for tq in range(QUERY_GROUP):
                    run = run_ref[tq, h]
                    last = tq if diagonal else KEY_GROUP - 1
                    ws = [None] * (last + 1)
                    for g in reversed(range(last + 1)):
                        w = jnp.exp(log_betas[tq][g] + inners[tq][g] + run)
                        if diagonal and g == tq:
                            w = jnp.where(vis, w, 0.0)
                        ws[g] = w.astype(BF16)
                        run = run + totals[tq][g]
                    run_ref[tq, h] = run
                    wrow = ws[0] if last == 0 else jnp.concatenate(ws, axis=1)
                    if diagonal:
                        acc_ref[tq, h] += _dot(wrow, vrows[0:(last + 1) * LANES, :])
                    else:
                        weights.append(wrow)
                if not diagonal:
                    pv = _dot(jnp.concatenate(weights, axis=0), vrows)
                    for tq in range(QUERY_GROUP):
                        acc_ref[tq, h] += pv[tq * LANES:(tq + 1) * LANES]

        group(a, True)

        def lower(t, c):
            group(a - 1 - t, False)
            return c

        lax.fori_loop(0, a, lower, 0)
        for tq in range(QUERY_GROUP):
            o = jnp.where(lane < SB_HEAD_DIM, acc_ref[tq, 0], acc_ref[tq, 1])
            o_ref[0, pl.ds(r_base + tq * LANES, LANES), :] = o.astype(o_ref.dtype)
        return carry

    def full_tile(i, carry):
        r0 = pl.multiple_of(i * LANES, LANES)
        o = _sb_tile(i, LANES, (qas[pl.ds(r0, LANES), :], qbs[pl.ds(r0, LANES), :]), ks, vs, suffix)
        o_ref[0, pl.ds(r0, LANES), :] = o.astype(o_ref.dtype)
        return carry

    n_super = n_full // QUERY_GROUP
    lax.fori_loop(0, n_super, super_tile, 0)
    lax.fori_loop(n_super * QUERY_GROUP, n_full, full_tile, 0)
    if tail:
        r0 = n_full * LANES
        o = _sb_rows_static(n_full, tail, (qas[r0:r0 + tail, :], qbs[r0:r0 + tail, :]), ks, vs, suffix)
        o_ref[0, r0:r0 + tail, :] = o.astype(o_ref.dtype)


def _sb_prompt(q, k, v):
    b, seq, _ = q.shape
    assert seq % BF16_ROWS == 0
    lp = pl.cdiv(pl.cdiv(seq, LANES), KEY_GROUP) * GROUP_KEYS
    blk = lambda: pl.BlockSpec((1, seq, LANES), lambda i, p: (i, 0, p))
    return pl.pallas_call(
        functools.partial(_sb_prompt_kernel, seq=seq),
        grid=(b, SB_HEADS // 2),
        in_specs=[blk(), blk(), blk()],
        out_specs=blk(),
        out_shape=jax.ShapeDtypeStruct((b, seq, SB_HEADS * SB_HEAD_DIM), BF16),
        scratch_shapes=[pltpu.VMEM((seq, LANES), BF16), pltpu.VMEM((seq, LANES), BF16),
                        pltpu.VMEM((lp, LANES), BF16), pltpu.VMEM((lp, LANES), BF16),
                        pltpu.VMEM((QUERY_GROUP, 2, LANES, LANES), F32),
                        pltpu.VMEM((QUERY_GROUP, 2, LANES, LANES), F32)],
        compiler_params=_params(("arbitrary", "arbitrary")),
        name="stick_prompt",
    )(q, k, v)


def _sb_sample_kernel(q_ref, k_ref, v_ref, ck_ref, cv_ref, o_ref, ks, vs, *, past, n_new):
    lw = ks.shape[0]
    tot = past + n_new
    slab = pl.ds(pl.program_id(1), past, stride=CACHE_SLABS)
    ks[0:past, :] = ck_ref[0, 0, slab, :].astype(BF16)
    vs[0:past, :] = cv_ref[0, 0, slab, :].astype(BF16)
    ks[past:tot, :] = k_ref[0].astype(BF16)
    vs[past:tot, :] = v_ref[0].astype(BF16)
    if lw > tot:
        ks[tot:lw, :] = jnp.zeros((lw - tot, LANES), BF16)
        vs[tot:lw, :] = jnp.zeros((lw - tot, LANES), BF16)
    o = _sb_rows_static(past // LANES, n_new, _split_heads(q_ref[0]), ks, vs, _suffix_matrix())
    o_ref[0] = o.astype(o_ref.dtype)


def _sb_sample(q, k, v, cache_k, cache_v, layer):
    b, n_new, _ = q.shape
    past = cache_k.shape[2] // CACHE_SLABS
    assert past % LANES == 0 and n_new <= LANES and n_new % BF16_ROWS == 0
    lw = (past // GROUP_KEYS + 1) * GROUP_KEYS
    new = lambda: pl.BlockSpec((1, n_new, LANES), lambda i, p: (i, 0, p))
    old = lambda: pl.BlockSpec((1, 1, past * CACHE_SLABS, LANES), lambda i, p: (layer, i, 0, 0))
    return pl.pallas_call(
        functools.partial(_sb_sample_kernel, past=past, n_new=n_new),
        grid=(b, SB_HEADS // 2),
        in_specs=[new(), new(), new(), old(), old()],
        out_specs=new(),
        out_shape=jax.ShapeDtypeStruct((b, n_new, SB_HEADS * SB_HEAD_DIM), BF16),
        scratch_shapes=[pltpu.VMEM((lw, LANES), BF16), pltpu.VMEM((lw, LANES), BF16)],
        compiler_params=_params(("arbitrary", "arbitrary")),
        name="stick_sample",
    )(q, k, v, cache_k, cache_v)


def _ssd_kernel(xbc_ref, z_ref, dt_ref, hist_ref, h0_ref, cw_ref, cb_ref, dtb_ref, alog_ref,
                dexp_ref, gn_ref, y_ref, hfin_ref, prev8, ht, *, seq):
    d_in = SSM_HEADS * SSM_HEAD_DIM
    gw = d_in // SSM_GROUPS
    n_full, tail = divmod(seq, SSD_BLOCK)
    blk = SSD_BLOCK

    r = lax.broadcasted_iota(jnp.int32, (blk, blk), 0)
    c = lax.broadcasted_iota(jnp.int32, (blk, blk), 1)
    tril = r >= c
    tri = jnp.where(tril, 1.0, 0.0).astype(BF16)
    er = lax.broadcasted_iota(jnp.int32, (LANES, d_in), 0)
    ec = lax.broadcasted_iota(jnp.int32, (LANES, d_in), 1)
    expand = jnp.where(jnp.right_shift(ec, 6) == er, 1.0, 0.0).astype(BF16)
    lane = lax.broadcasted_iota(jnp.int32, (1, LANES), 1)

    prev8[...] = hist_ref[0]
    for g in range(SSM_GROUPS):
        for j in range(gw // LANES):
            rows = h0_ref[0, g * gw + j * LANES:g * gw + (j + 1) * LANES, :]
            ht[g, :, j * LANES:(j + 1) * LANES] = rows.T

    a_neg = -jnp.exp(alog_ref[...])
    cw = cw_ref[...]

    def block(t0, rows):
        xin = xbc_ref[0, pl.ds(t0, rows), :]
        win = jnp.concatenate([prev8[...], xin], axis=0)
        conv = cw[3:4] * win
        for k in range(1, SSM_CONV):
            conv = conv + cw[3 - k:4 - k] * pltpu.roll(win, k, axis=0)
        xbc = _silu(conv[SUBLANES:] + cb_ref[...])
        prev8[...] = xin[rows - SUBLANES:rows]
        dt = _softplus(dt_ref[0, pl.ds(t0, rows), :] + dtb_ref[...])
        z = z_ref[0, pl.ds(t0, rows), :]
        if rows < blk:
            pad = blk - rows
            xbc = jnp.concatenate([xbc, jnp.zeros((pad, xbc.shape[1]), F32)], axis=0)
            dt = jnp.concatenate([dt, jnp.zeros((pad, LANES), F32)], axis=0)
            z = jnp.concatenate([z, jnp.zeros((pad, d_in), F32)], axis=0)
        x = xbc[:, 0:d_in]
        cs = _dot_exact_lhs(tri, dt * a_neg)
        cs_t = cs.T
        dt_e = _dot_exact_rhs(dt, expand)
        cs_e = _dot_exact_rhs(cs, expand)
        xdt = x * dt_e
        xdt_b = xdt.astype(BF16)
        grow = jnp.exp(cs_e)
        cs_last = cs_e[blk - 1:blk, :]
        to_end = (jnp.exp(cs_last - cs_e) * xdt).astype(BF16)
        carry_decay = jnp.exp(cs_last)
        ys = []
        for g in range(SSM_GROUPS):
            bmat = xbc[:, d_in + g * SSM_D_STATE:d_in + (g + 1) * SSM_D_STATE]
            cmat = xbc[:, d_in + (SSM_GROUPS + g) * SSM_D_STATE:d_in + (SSM_GROUPS + g + 1) * SSM_D_STATE]
            cmat_b = cmat.astype(BF16)
            cb = _dot_nt(cmat_b, bmat.astype(BF16))
            hprev = ht[g]
            y_g = _dot(cmat_b, hprev.astype(BF16)) * grow[:, g * gw:(g + 1) * gw]
            parts = []
            for pr in range(gw // LANES):
                l0 = g * gw + pr * LANES
                slab = xdt_b[:, l0:l0 + LANES]
                hd = l0 // SSM_HEAD_DIM
                res = []
                for hh in (hd, hd + 1):
                    decay = jnp.exp(jnp.where(tril, cs[:, hh:hh + 1] - cs_t[hh:hh + 1, :], -jnp.inf))
                    res.append(_dot((cb * decay).astype(BF16), slab))
                parts.append(jnp.where(lane < SSM_HEAD_DIM, res[0], res[1]))
            y_g = y_g + jnp.concatenate(parts, axis=1)
            ht[g] = carry_decay[:, g * gw:(g + 1) * gw] * hprev + _dot(bmat.T.astype(BF16), to_end[:, g * gw:(g + 1) * gw])
            ys.append(y_g)
        y = jnp.concatenate(ys, axis=1)
        y = (y + dexp_ref[...] * x) * _silu(z)
        outs = []
        for g in range(SSM_GROUPS):
            outs.append(_rms(y[:, g * gw:(g + 1) * gw], gn_ref[:, g * gw:(g + 1) * gw]))
        y = jnp.concatenate(outs, axis=1)
        y_ref[0, pl.ds(t0, rows), :] = y[0:rows].astype(y_ref.dtype)

    def full_block(i, carry):
        block(pl.multiple_of(i * blk, blk), blk)
        return carry

    lax.fori_loop(0, n_full, full_block, 0)
    if tail:
        block(n_full * blk, tail)
    for g in range(SSM_GROUPS):
        for j in range(gw // LANES):
            hfin_ref[0, g * gw + j * LANES:g * gw + (j + 1) * LANES, :] = ht[g, :, j * LANES:(j + 1) * LANES].T


def _ssd(xbc, z, dt, hist8, h0, cw, cb, dtb, alog, dexp, gn):
    b, seq, cdim = xbc.shape
    d_in = SSM_HEADS * SSM_HEAD_DIM
    assert seq % SUBLANES == 0 and (seq % SSD_BLOCK) % BF16_ROWS == 0
    per_b = lambda n: pl.BlockSpec((1, seq, n), lambda i: (i, 0, 0))
    const = lambda shape: pl.BlockSpec(shape, lambda i: (0,) * len(shape))
    return pl.pallas_call(
        functools.partial(_ssd_kernel, seq=seq),
        grid=(b,),
        in_specs=[per_b(cdim), per_b(d_in), per_b(LANES),
                  pl.BlockSpec((1, SUBLANES, cdim), lambda i: (i, 0, 0)),
                  pl.BlockSpec((1, d_in, SSM_D_STATE), lambda i: (i, 0, 0)),
                  const((SSM_CONV, cdim)), const((1, cdim)), const((1, LANES)), const((1, LANES)),
                  const((1, d_in)), const((1, d_in))],
        out_specs=[per_b(d_in), pl.BlockSpec((1, d_in, SSM_D_STATE), lambda i: (i, 0, 0))],
        out_shape=[jax.ShapeDtypeStruct((b, seq, d_in), BF16),
                   jax.ShapeDtypeStruct((b, d_in, SSM_D_STATE), F32)],
        scratch_shapes=[pltpu.VMEM((SUBLANES, cdim), F32),
                        pltpu.VMEM((SSM_GROUPS, SSM_D_STATE, d_in // SSM_GROUPS), F32)],
        compiler_params=_params(("arbitrary",)),
        name="ssd",
    )(xbc, z, dt, hist8, h0, cw, cb, dtb, alog, dexp, gn)


def _out_ffn_kernel(x_ref, da_ref, y_ref, sb_ref, wo_ref, g_ref, wg_ref, wu_ref, wd_ref,
                    fw_ref, fb_ref, init_ref, o_ref, last_ref, act_ref):
    ts = x_ref.shape[1]
    d_ff = wg_ref.shape[1]
    nd = da_ref.shape[2]
    ny = y_ref.shape[2]

    @pl.when(pl.program_id(1) == 0)
    def _():
        last_ref[...] = init_ref[...]

    xr = (x_ref[0] + _dot(da_ref[0], wo_ref[0:nd, :]) + _dot(y_ref[0], wo_ref[nd:nd + ny, :])
          + _dot(sb_ref[0], wo_ref[nd + ny:, :]))
    h2 = _rms(xr, g_ref[...]).astype(BF16)
    row8 = lax.broadcasted_iota(jnp.int32, (SUBLANES, FF_CHUNK), 0)
    for c0 in range(0, d_ff, FF_CHUNK):
        cols = slice(c0, c0 + FF_CHUNK)
        g = _dot(h2, wg_ref[:, cols])
        u = _dot(h2, wu_ref[:, cols])
        hist = last_ref[0, :, cols]
        conv = fw_ref[2:3, cols] * g + fb_ref[:, cols]
        for k in (1, 2):
            shifted = pltpu.roll(g, k, axis=0)
            top = jnp.where(row8 < k, pltpu.roll(hist, k, axis=0), shifted[0:SUBLANES])
            shifted = jnp.concatenate([top, shifted[SUBLANES:]], axis=0)
            conv = conv + fw_ref[2 - k:3 - k, cols] * shifted
        last_ref[0, :, cols] = g[ts - SUBLANES:ts]
        act_ref[:, cols] = (_silu(conv) * u).astype(BF16)
    o_ref[0] = xr + _dot(act_ref[...], wd_ref[...])


def _out_ffn(x, da_o, y, sb_o, wo, g, wg, wu, wd, fw, fb, init8):
    b, seq, d = x.shape
    d_ff = wg.shape[1]
    assert d_ff % FF_CHUNK == 0
    ts = _pick_tile(seq, 704, BF16_ROWS)
    row = lambda n: pl.BlockSpec((1, ts, n), lambda i, j: (i, j, 0))
    state = lambda: pl.BlockSpec((1, SUBLANES, d_ff), lambda i, j: (i, 0, 0))
    return pl.pallas_call(
        _out_ffn_kernel,
        grid=(b, seq // ts),
        in_specs=[row(d), row(da_o.shape[2]), row(y.shape[2]), row(sb_o.shape[2]),
                  _resident(wo.shape), _resident((1, d)), _resident(wg.shape), _resident(wu.shape),
                  _resident(wd.shape), _resident(fw.shape), _resident((1, d_ff)), state()],
        out_specs=[row(d), state()],
        out_shape=[jax.ShapeDtypeStruct((b, seq, d), F32),
                   jax.ShapeDtypeStruct((b, SUBLANES, d_ff), F32)],
        scratch_shapes=[pltpu.VMEM((ts, d_ff), BF16)],
        compiler_params=_params(("arbitrary", "arbitrary")),
        name="out_ffn",
    )(x, da_o, y, sb_o, wo, g, wg, wu, wd, fw, fb, init8)


def _final_kernel(x_ref, g_ref, o_ref, *, skip):
    n = o_ref.shape[1]
    o_ref[0] = _rms(x_ref[0, skip:skip + n, :], g_ref[...])


def _final_norm(x, g, skip):
    b, seq, d = x.shape
    n = seq - skip
    assert skip % SUBLANES == 0
    return pl.pallas_call(
        functools.partial(_final_kernel, skip=skip),
        grid=(b,),
        in_specs=[pl.BlockSpec((1, seq, d), lambda i: (i, 0, 0)), pl.BlockSpec((1, d), lambda i: (0, 0))],
        out_specs=pl.BlockSpec((1, n, d), lambda i: (i, 0, 0)),
        out_shape=jax.ShapeDtypeStruct((b, n, d), F32),
        compiler_params=_params(("arbitrary",)),
        name="final_norm",
    )(x, g)


def _pad_rows_top(a, rows):
    return jnp.pad(a, ((0, 0), (rows - a.shape[1], 0), (0, 0)))


def _layer(x, past, layer, p, lam_init, bias_prompt, bias_sample):
    b, seq, d = x.shape
    proj = _in_proj(x.reshape(b * seq, d), p["norm_mix"], p["w_in"])
    da_q, da_k, da_v, ssm_z, ssm_xbc, sb_q, sb_k, sb_v, ssm_dt = [a.reshape(b, seq, -1) for a in proj]
    if past is None:
        da_o = _da_prompt(da_q, da_k, da_v, bias_prompt, p["lamp"], p["da_subln"], lam_init)
        sb_o = _sb_prompt(sb_q, sb_k, sb_v)
        hist8 = jnp.zeros((b, SUBLANES, ssm_xbc.shape[2]), F32)
        h0 = jnp.zeros((b, SSM_HEADS * SSM_HEAD_DIM, SSM_D_STATE), F32)
        ffn_init = jnp.zeros((b, SUBLANES, p["w_gate"].shape[1]), F32)
    else:
        c_da_k, c_da_v, c_sb_k, c_sb_v, state_ssm, state_conv, state_ffn = past
        da_o = _da_sample(da_q, da_k, da_v, c_da_k, c_da_v, layer, bias_sample, p["lamp"], p["da_subln"], lam_init)
        sb_o = _sb_sample(sb_q, sb_k, sb_v, c_sb_k, c_sb_v, layer)
        hist8 = _pad_rows_top(state_conv[layer], SUBLANES)
        h0 = state_ssm[layer].reshape(b, SSM_HEADS * SSM_HEAD_DIM, SSM_D_STATE)
        ffn_init = _pad_rows_top(state_ffn[layer], SUBLANES)
    y, h_final = _ssd(ssm_xbc, ssm_z, ssm_dt, hist8, h0, p["conv_w"], p["conv_b"], p["dt_bias"],
                      p["a_log"], p["d_exp"], p["ssm_norm"])
    x_new, ffn_last = _out_ffn(x, da_o, y, sb_o, p["w_out"], p["norm_ffn"], p["w_gate"], p["w_up"],
                               p["w_down"], p["fconv_w"], p["fconv_b"], ffn_init)
    conv_in_tail = jnp.concatenate([hist8, ssm_xbc[:, -min(seq, SUBLANES):]], axis=1)
    states = (da_k.reshape(b, seq, DA_HEADS, 2, DA_HEAD_DIM),
              da_v.reshape(b, seq, DA_HEADS, DA_VDIM),
              sb_k.reshape(b, seq, SB_HEADS, SB_HEAD_DIM),
              sb_v.reshape(b, seq, SB_HEADS, SB_HEAD_DIM),
              h_final.reshape(b, SSM_HEADS, SSM_HEAD_DIM, SSM_D_STATE),
              conv_in_tail[:, -(SSM_CONV - 1):],
              ffn_last[:, -(FFN_CONV - 1):])
    return x_new, states


def _pad_lanes(v, n):
    return jnp.pad(v, (0, n - v.shape[0])).reshape(1, n)


def kernel(x_prompt, x_sample, cache_da_k, cache_da_v, cache_sb_k, cache_sb_v, state_ssm, state_ssm_conv, state_ffn_conv, meta_tokens, rel_bias_table, w_in, w_out, norm_mix, norm_ffn, da_lambda_q1, da_lambda_k1, da_lambda_q2, da_lambda_k2, da_subln, ssm_conv_w, ssm_conv_b, ssm_dt_bias, ssm_a_log, ssm_d, ssm_norm, ffn_w_gate, ffn_w_up, ffn_w_down, ffn_conv_w, ffn_conv_b, final_norm):
    depth = w_in.shape[0]
    b = x_prompt.shape[0]
    d = x_prompt.shape[2]
    sb, n_new, _ = x_sample.shape
    past_len = cache_da_k.shape[2]
    n_meta = meta_tokens.shape[0]
    assert n_meta == N_META

    xp = jnp.concatenate([jnp.broadcast_to(meta_tokens[None], (b, n_meta, d)), x_prompt], axis=1)
    xs = x_sample
    bias_prompt, bias_sample = _bias_tiles(rel_bias_table, past_len, n_new)
    slab_rows = lambda c: c.reshape(depth, sb, past_len * CACHE_SLABS, LANES)
    cache = (slab_rows(cache_da_k), slab_rows(cache_da_v), slab_rows(cache_sb_k), slab_rows(cache_sb_v),
             state_ssm, state_ssm_conv, state_ffn_conv)

    p_states, s_states = [], []
    for i in range(depth):
        wi = w_in[i]
        w_r = jnp.concatenate([wi[:, :_OFF_DT], wi[:, _OFF_DT + SSM_HEADS:],
                               jnp.pad(wi[:, _OFF_DT:_OFF_DT + SSM_HEADS], ((0, 0), (0, LANES - SSM_HEADS)))],
                              axis=1).astype(BF16)
        p = dict(
            w_in=w_r, w_out=w_out[i].astype(BF16),
            norm_mix=norm_mix[i].reshape(1, d), norm_ffn=norm_ffn[i].reshape(1, d),
            lamp=jnp.stack([da_lambda_q1[i], da_lambda_k1[i], da_lambda_q2[i], da_lambda_k2[i]]),
            da_subln=da_subln[i].reshape(1, DA_VDIM),
            conv_w=ssm_conv_w[i], conv_b=ssm_conv_b[i].reshape(1, -1),
            dt_bias=_pad_lanes(ssm_dt_bias[i], LANES), a_log=_pad_lanes(ssm_a_log[i], LANES),
            d_exp=jnp.repeat(ssm_d[i], SSM_HEAD_DIM).reshape(1, -1), ssm_norm=ssm_norm[i].reshape(1, -1),
            w_gate=ffn_w_gate[i].astype(BF16), w_up=ffn_w_up[i].astype(BF16), w_down=ffn_w_down[i].astype(BF16),
            fconv_w=ffn_conv_w[i], fconv_b=ffn_conv_b[i].reshape(1, -1),
        )
        lam_init = 0.8 - 0.6 * math.exp(-0.3 * i)
        xp, st_p = _layer(xp, None, i, p, lam_init, bias_prompt, bias_sample)
        xs, st_s = _layer(xs, cache, i, p, lam_init, bias_prompt, bias_sample)
        p_states.append(st_p)
        s_states.append(st_s)

    g_fin = final_norm.reshape(1, d)
    y_prompt = _final_norm(xp, g_fin, n_meta)
    y_sample = _final_norm(xs, g_fin, 0)
    stack = lambda states: tuple(jnp.stack([s[j] for s in states], axis=0) for j in range(7))
    return (y_prompt, y_sample) + stack(p_states) + stack(s_states)
```
